```python
import jax, jax.numpy as jnp
from jax import lax
import numpy as np

D_MODEL = 1024
BATCH = 2
SEQ = 16384
DEPTH = 1
DEC_BATCH = 128
DEC_SEQ = 4
PAST_LEN = 8192
PAGE_SIZE = 128

N_META = 16
FOX_HEADS = 8
FOX_HEAD_DIM = 64
FOX_SCALE = FOX_HEAD_DIM ** -0.5
Q_BLOCK = 128
GLA_HEADS = 4
GLA_KEY_DIM = 64
GLA_VAL_DIM = 128
GLA_GATE_RANK = 16
GLA_GATE_TEMP = 16.0
GLA_CHUNK = 64
D_FF = 4 * D_MODEL
D_MIX = FOX_HEADS * FOX_HEAD_DIM + GLA_HEADS * GLA_VAL_DIM
EPS = 1e-6
SPLITS = (FOX_HEADS * FOX_HEAD_DIM, FOX_HEADS * FOX_HEAD_DIM, FOX_HEADS * FOX_HEAD_DIM, FOX_HEADS,
          GLA_HEADS * GLA_KEY_DIM, GLA_HEADS * GLA_KEY_DIM, GLA_HEADS * GLA_VAL_DIM, GLA_HEADS * GLA_VAL_DIM,
          GLA_GATE_RANK)
D_IN_PROJ = sum(SPLITS)

kernel_name = "hymba_fox_gla_decoder_step"


def _rmsnorm(x, g):
    xf = x.astype(jnp.float32)
    return (xf * lax.rsqrt(jnp.mean(xf * xf, axis=-1, keepdims=True) + EPS) * g).astype(x.dtype)


def _split_cols(z):
    out, start = [], 0
    for w in SPLITS:
        out.append(z[..., start:start + w])
        start += w
    return out


def _mixer_inputs(xn, w_in, fox_b_f, fox_q_norm_g, fox_k_norm_g, gla_w_gup, gla_b_g):
    z = xn @ w_in
    fq, fk, fv, ff, gq, gk, gv, gr, glr = _split_cols(z)
    lead = z.shape[:-1]
    fq = _rmsnorm(fq.reshape(*lead, FOX_HEADS, FOX_HEAD_DIM), fox_q_norm_g)
    fk = _rmsnorm(fk.reshape(*lead, FOX_HEADS, FOX_HEAD_DIM), fox_k_norm_g)
    fv = fv.reshape(*lead, FOX_HEADS, FOX_HEAD_DIM)
    logf = jax.nn.log_sigmoid((ff + fox_b_f).astype(jnp.float32))
    gq = gq.reshape(*lead, GLA_HEADS, GLA_KEY_DIM) * (GLA_KEY_DIM ** -0.5)
    gk = gk.reshape(*lead, GLA_HEADS, GLA_KEY_DIM)
    gv = gv.reshape(*lead, GLA_HEADS, GLA_VAL_DIM)
    gg = (jax.nn.log_sigmoid((glr @ gla_w_gup + gla_b_g).astype(jnp.float32)) / GLA_GATE_TEMP)
    gg = gg.reshape(*lead, GLA_HEADS, GLA_KEY_DIM)
    return fq, fk, fv, logf, gq, gk, gv, gr, gg


def _fox_attend(q, k, v, f_q, f_k, q_pos, k_pos):
    s = jnp.einsum('bqhd,bkhd->bhqk', q, k).astype(jnp.float32) * FOX_SCALE
    s = s + jnp.moveaxis(f_q, 1, 2)[..., :, None] - jnp.moveaxis(f_k, 1, 2)[..., None, :]
    s = jnp.where(k_pos[None, :] <= q_pos[:, None], s, -jnp.inf)
    p = jax.nn.softmax(s, axis=-1)
    return jnp.einsum('bhqk,bkhd->bqhd', p.astype(v.dtype), v)


def _gla_chunked(q, k, v, g, s0, chunk):
    b, l, h, dk = q.shape
    n = l // chunk
    mask = jnp.tril(jnp.ones((chunk, chunk), dtype=bool))[None, :, :, None, None]

    def to_chunks(a):
        return jnp.moveaxis(a.reshape(b, n, chunk, *a.shape[2:]), 1, 0)

    def step(s, inp):
        qc, kc, vc, gc = inp
        bc = jnp.cumsum(gc, axis=1)
        diff = bc[:, :, None] - bc[:, None, :]
        decay = jnp.exp(jnp.where(mask, diff, -jnp.inf))
        a = jnp.einsum('bthd,bshd,btshd->bhts', qc, kc, decay)
        o = jnp.einsum('bhts,bshv->bthv', a, vc) + jnp.einsum('bthd,bhdv->bthv', qc * jnp.exp(bc), s)
        b_last = bc[:, -1]
        s = jnp.exp(b_last)[..., None] * s + jnp.einsum('bshd,bshv->bhdv', kc * jnp.exp(b_last[:, None] - bc), vc)
        return s, o

    s_fin, o = lax.scan(step, s0, (to_chunks(q), to_chunks(k), to_chunks(v), to_chunks(g)))
    o = jnp.moveaxis(o, 0, 1).reshape(b, l, h, v.shape[-1])
    return o, s_fin


def _mixer_output(x, o_fox, o_gla, gr, gla_out_norm_g, w_out):
    lead = x.shape[:2]
    o_g = _rmsnorm(o_gla, gla_out_norm_g) * jax.nn.silu(gr.reshape(*lead, GLA_HEADS, GLA_VAL_DIM))
    o = jnp.concatenate([o_fox.reshape(*lead, -1).astype(x.dtype), o_g.reshape(*lead, -1).astype(x.dtype)], axis=-1)
    return x + o @ w_out


def _mlp(x, norm_mlp_g, w_up, w_down):
    hn = _rmsnorm(x, norm_mlp_g)
    return x + jnp.square(jax.nn.relu(hn @ w_up)) @ w_down


def _layer(xp, xs, cache_k, cache_v, cache_logf, state_gla, page_table, norm_mix_g, w_in, fox_b_f,
           fox_q_norm_g, fox_k_norm_g, gla_w_gup, gla_b_g, gla_out_norm_g, w_out, norm_mlp_g, w_up, w_down):
    b, t, _ = xp.shape
    l_real = t - N_META
    xn = _rmsnorm(xp, norm_mix_g)
    fq, fk, fv, logf, gq, gk, gv, gr, gg = _mixer_inputs(xn, w_in, fox_b_f, fox_q_norm_g, fox_k_norm_g, gla_w_gup, gla_b_g)
    f_cum = jnp.cumsum(logf, axis=1)
    pos = jnp.arange(t)
    o_meta = _fox_attend(fq[:, :N_META], fk[:, :N_META], fv[:, :N_META], f_cum[:, :N_META], f_cum[:, :N_META],
                         pos[:N_META], pos[:N_META])

    def q_block(start):
        q_i = lax.dynamic_slice_in_dim(fq, start, Q_BLOCK, axis=1)
        f_i = lax.dynamic_slice_in_dim(f_cum, start, Q_BLOCK, axis=1)
        return _fox_attend(q_i, fk, fv, f_i, f_cum, start + jnp.arange(Q_BLOCK), pos)

    starts = N_META + Q_BLOCK * jnp.arange(l_real // Q_BLOCK)
    o_real = lax.map(q_block, starts)
    o_real = jnp.moveaxis(o_real, 0, 1).reshape(b, l_real, FOX_HEADS, FOX_HEAD_DIM)
    o_fox_p = jnp.concatenate([o_meta, o_real], axis=1)

    s0 = jnp.zeros((b, GLA_HEADS, GLA_KEY_DIM, GLA_VAL_DIM), jnp.float32)
    o_gm, s_meta = _gla_chunked(gq[:, :N_META], gk[:, :N_META], gv[:, :N_META], gg[:, :N_META], s0, N_META)
    o_gr, s_prompt = _gla_chunked(gq[:, N_META:], gk[:, N_META:], gv[:, N_META:], gg[:, N_META:], s_meta, GLA_CHUNK)
    o_gla_p = jnp.concatenate([o_gm, o_gr], axis=1)
    hp = _mixer_output(xp, o_fox_p, o_gla_p, gr, gla_out_norm_g, w_out)
    yp = _mlp(hp, norm_mlp_g, w_up, w_down)

    dec_seq = xs.shape[1]
    xsn = _rmsnorm(xs, norm_mix_g)
    sq, sk, sv, slogf, hq, hk, hv, hr, hg = _mixer_inputs(xsn, w_in, fox_b_f, fox_q_norm_g, fox_k_norm_g, gla_w_gup, gla_b_g)

    def one_seq(args):
        q, k, v, lf, pages = args
        past = pages.shape[0] * PAGE_SIZE
        k_all = jnp.concatenate([cache_k[pages].reshape(past, FOX_HEADS, FOX_HEAD_DIM), k], axis=0)
        v_all = jnp.concatenate([cache_v[pages].reshape(past, FOX_HEADS, FOX_HEAD_DIM), v], axis=0)
        lf_all = jnp.concatenate([cache_logf[pages].reshape(past, FOX_HEADS).astype(jnp.float32), lf], axis=0)
        f_all = jnp.cumsum(lf_all, axis=0)
        k_pos = jnp.arange(past + dec_seq)
        o = _fox_attend(q[None], k_all[None], v_all[None], f_all[None, past:], f_all[None], k_pos[past:], k_pos)
        return o[0]

    o_fox_s = lax.map(one_seq, (sq, sk, sv, slogf, page_table))
    o_gla_s, s_sample = _gla_chunked(hq, hk, hv, hg, state_gla.astype(jnp.float32), dec_seq)
    hs = _mixer_output(xs, o_fox_s, o_gla_s, hr, gla_out_norm_g, w_out)
    ys = _mlp(hs, norm_mlp_g, w_up, w_down)
    return yp, ys, fk, fv, logf, s_prompt, sk, sv, slogf, s_sample


def setup_inputs(seed: int = 0) -> dict:
    key = jax.random.key(seed)
    ks = jax.random.split(key, 24)
    n_pages = PAST_LEN // PAGE_SIZE
    n_pool = (5 * DEC_BATCH * n_pages) // 4
    nrm = jax.random.normal
    page_table = jax.random.permutation(ks[0], n_pool)[:DEC_BATCH * n_pages].reshape(DEC_BATCH, n_pages).astype(jnp.int32)
    return {
        "x_prompt": nrm(ks[1], (BATCH, SEQ, D_MODEL), jnp.float32),
        "x_sample": nrm(ks[2], (DEC_BATCH, DEC_SEQ, D_MODEL), jnp.float32),
        "cache_k": nrm(ks[3], (DEPTH, n_pool, PAGE_SIZE, FOX_HEADS, FOX_HEAD_DIM), jnp.float32),
        "cache_v": nrm(ks[4], (DEPTH, n_pool, PAGE_SIZE, FOX_HEADS, FOX_HEAD_DIM), jnp.float32),
        "cache_logf": jax.nn.log_sigmoid(2.5 + nrm(ks[5], (DEPTH, n_pool, PAGE_SIZE, FOX_HEADS), jnp.float32)),
        "state_gla": 0.3 * nrm(ks[6], (DEPTH, DEC_BATCH, GLA_HEADS, GLA_KEY_DIM, GLA_VAL_DIM), jnp.float32),
        "page_table": page_table,
        "meta_tokens": nrm(ks[7], (N_META, D_MODEL), jnp.float32),
        "norm_mix_g": 1.0 + 0.05 * nrm(ks[8], (DEPTH, D_MODEL), jnp.float32),
        "w_in": nrm(ks[9], (DEPTH, D_MODEL, D_IN_PROJ), jnp.float32) * D_MODEL ** -0.5,
        "fox_b_f": jax.random.uniform(ks[10], (DEPTH, FOX_HEADS), jnp.float32, 1.0, 4.0),
        "fox_q_norm_g": 1.0 + 0.05 * nrm(ks[11], (DEPTH, FOX_HEAD_DIM), jnp.float32),
        "fox_k_norm_g": 1.0 + 0.05 * nrm(ks[12], (DEPTH, FOX_HEAD_DIM), jnp.float32),
        "gla_w_gup": nrm(ks[13], (DEPTH, GLA_GATE_RANK, GLA_HEADS * GLA_KEY_DIM), jnp.float32) * GLA_GATE_RANK ** -0.5,
        "gla_b_g": 0.1 * nrm(ks[14], (DEPTH, GLA_HEADS * GLA_KEY_DIM), jnp.float32),
        "gla_out_norm_g": 1.0 + 0.05 * nrm(ks[15], (DEPTH, GLA_VAL_DIM), jnp.float32),
        "w_out": nrm(ks[16], (DEPTH, D_MIX, D_MODEL), jnp.float32) * D_MIX ** -0.5,
        "norm_mlp_g": 1.0 + 0.05 * nrm(ks[17], (DEPTH, D_MODEL), jnp.float32),
        "w_up": nrm(ks[18], (DEPTH, D_MODEL, D_FF), jnp.float32) * D_MODEL ** -0.5,
        "w_down": nrm(ks[19], (DEPTH, D_FF, D_MODEL), jnp.float32) * D_FF ** -0.5,
    }


def reference(x_prompt, x_sample, cache_k, cache_v, cache_logf, state_gla, page_table, meta_tokens, norm_mix_g,
              w_in, fox_b_f, fox_q_norm_g, fox_k_norm_g, gla_w_gup, gla_b_g, gla_out_norm_g, w_out, norm_mlp_g,
              w_up, w_down):
    b = x_prompt.shape[0]
    meta = jnp.broadcast_to(meta_tokens[None].astype(x_prompt.dtype), (b, N_META, D_MODEL))
    xp = jnp.concatenate([meta, x_prompt], axis=1)
    xs = x_sample
    outs = [[] for _ in range(8)]
    for l in range(DEPTH):
        xp, xs, *st = _layer(xp, xs, cache_k[l], cache_v[l], cache_logf[l], state_gla[l], page_table,
                             norm_mix_g[l], w_in[l], fox_b_f[l], fox_q_norm_g[l], fox_k_norm_g[l], gla_w_gup[l],
                             gla_b_g[l], gla_out_norm_g[l], w_out[l], norm_mlp_g[l], w_up[l], w_down[l])
        for lst, a in zip(outs, st):
            lst.append(a)
    k_p, v_p, lf_p, s_p, k_s, v_s, lf_s, s_s = [jnp.stack(a) for a in outs]
    return (xp[:, N_META:], xs, k_p, v_p, lf_p, s_p, k_s, v_s, lf_s, s_s)
```

```python
import functools

import jax
import jax.numpy as jnp
import numpy as np
from jax import lax
from jax.experimental import pallas as pl
from jax.experimental.pallas import tpu as pltpu

D_MODEL = 1024
N_META = 16
PAGE_SIZE = 128
FOX_HEADS = 8
FOX_HEAD_DIM = 64
FOX_DIM = FOX_HEADS * FOX_HEAD_DIM
GLA_HEADS = 4
GLA_KEY_DIM = 64
GLA_VAL_DIM = 128
GLA_KDIM = GLA_HEADS * GLA_KEY_DIM
GLA_VDIM = GLA_HEADS * GLA_VAL_DIM
GLA_GATE_RANK = 16
GLA_GATE_TEMP = 16.0
GLA_CHUNK = 64
D_FF = 4 * D_MODEL
EPS = 1e-6
LOG2E = 1.4426950408889634

LANES = 128
AUG_DIM = FOX_HEADS * LANES
SMALL_COLS = LANES
ROW_TILE = 512
DECODE_ROWS = 8
DECODE_PAGES_PER_STEP = 8
VMEM_LIMIT = 56 * 1024 * 1024

F32 = jnp.float32
BF16 = jnp.bfloat16

_C_FQ, _C_FK, _C_FV = 0, FOX_DIM, 2 * FOX_DIM
_C_GQ = 3 * FOX_DIM
_C_GK = _C_GQ + GLA_KDIM
_C_GV = _C_GK + GLA_KDIM
_C_GR = _C_GV + GLA_VDIM
_C_SM = _C_GR + GLA_VDIM
_W_COLS = _C_SM + SMALL_COLS


def _log_sigmoid(x):
    return jnp.minimum(x, 0.0) - jnp.log1p(jnp.exp(-jnp.abs(x)))


def _split3(x):
    p1 = x.astype(BF16)
    r = x - p1.astype(F32)
    p2 = r.astype(BF16)
    r = r - p2.astype(F32)
    return p1, p2, r.astype(BF16)


def _const_spec(shape):
    return pl.BlockSpec(shape, lambda *_: (0,) * len(shape), pipeline_mode=pl.Buffered(1))


def _inproj_kernel(x_ref, gmix_ref, w_ref, gmat_ref, qg_ref, kg_ref, bf_ref, wg_ref, bg_ref, *rest,
                   tm, t_real, seq_rows, prompt):
    if prompt:
        (tri_ref, pq_ref, pk_ref, pv_ref, oneq_ref, onek_ref, onev_ref,
         qa_ref, ka_ref, va_ref, ko_ref, vo_ref, lf_ref,
         gq_ref, gk_ref, gv_ref, gr_ref, gg_ref, carry_ref) = rest
    else:
        (qc_ref, ko_ref, vo_ref, lf_ref, gq_ref, gk_ref, gv_ref, gr_ref, gg_ref) = rest
    i = pl.program_id(1)

    x = x_ref[0]
    ms = jnp.mean(x * x, axis=-1, keepdims=True)
    xn = (x * lax.rsqrt(ms + EPS) * gmix_ref[...]).astype(BF16)
    z = jnp.dot(xn, w_ref[...], preferred_element_type=F32)

    def head_norm(zz, g_row):
        msq = jnp.dot((zz * zz).astype(BF16), gmat_ref[...], preferred_element_type=F32)
        return zz * lax.rsqrt(msq + EPS) * g_row

    qn = head_norm(z[:, _C_FQ:_C_FQ + FOX_DIM], qg_ref[...])
    kn = head_norm(z[:, _C_FK:_C_FK + FOX_DIM], kg_ref[...])
    zv = z[:, _C_FV:_C_FV + FOX_DIM]
    zs = z[:, _C_SM:_C_SM + SMALL_COLS]
    lf = _log_sigmoid(zs + bf_ref[...])
    if prompt:
        ko_ref[0] = kn.T
        vo_ref[0] = zv.T
        lf_ref[0] = lf.T[0:FOX_HEADS, :]
    else:
        ko_ref[0] = kn
        vo_ref[0] = zv
        lf_ref[0] = lf[:, 0:FOX_HEADS]

    row = i * tm + lax.broadcasted_iota(jnp.int32, (tm, 1), 0)
    if prompt:
        valid = row < t_real
    else:
        valid = (row % seq_rows) < t_real

    gq_ref[0] = z[:, _C_GQ:_C_GQ + GLA_KDIM] * (GLA_KEY_DIM ** -0.5)
    gk_ref[0] = z[:, _C_GK:_C_GK + GLA_KDIM]
    gv_ref[0] = z[:, _C_GV:_C_GV + GLA_VDIM]
    gr_ref[0] = z[:, _C_GR:_C_GR + GLA_VDIM]
    xg = jnp.dot(zs.astype(BF16), wg_ref[...], preferred_element_type=F32) + bg_ref[...]
    gg_ref[0] = jnp.where(valid, _log_sigmoid(xg) * (1.0 / GLA_GATE_TEMP), 0.0)

    if not prompt:
        qc_ref[0] = qn.astype(BF16)
        return

    @pl.when(i == 0)
    def _():
        carry_ref[...] = jnp.zeros_like(carry_ref)

    lane = lax.broadcasted_iota(jnp.int32, (tm, SMALL_COLS), 1)
    lfm = jnp.where(lane < FOX_HEADS, lf, 0.0)
    c = jnp.dot(tri_ref[...], jnp.concatenate(_split3(lfm), axis=1), preferred_element_type=F32)
    fcum = c[:, 0:LANES] + c[:, LANES:2 * LANES] + c[:, 2 * LANES:3 * LANES] + carry_ref[...]
    carry_ref[...] = fcum[tm - 1:tm, :]
    a1, a2, a3 = _split3(fcum * LOG2E)

    q_cat = jnp.concatenate([qn.astype(BF16), a1, a2, a3], axis=1)
    qa = jnp.dot(q_cat, pq_ref[...], preferred_element_type=F32) + oneq_ref[...]
    qa_ref[0] = qa.astype(BF16)
    k_cat = jnp.concatenate([kn.astype(BF16), -a1, -a2, -a3], axis=1)
    ka = jnp.dot(k_cat, pk_ref[...], preferred_element_type=F32) + onek_ref[...]
    ka_ref[0] = ka.astype(BF16)
    va = jnp.dot(zv.astype(BF16), pv_ref[...], preferred_element_type=F32) + onev_ref[...]
    va_ref[0] = va.astype(BF16)


def _placement_constants():
    n_cat = FOX_DIM + 3 * LANES
    pq = np.zeros((n_cat, AUG_DIM), np.float32)
    pk = np.zeros((n_cat, AUG_DIM), np.float32)
    pv = np.zeros((FOX_DIM, AUG_DIM), np.float32)
    oneq = np.zeros((1, AUG_DIM), np.float32)
    onek = np.zeros((1, AUG_DIM), np.float32)
    onev = np.zeros((1, AUG_DIM), np.float32)
    for h in range(FOX_HEADS):
        base = h * LANES
        for d in range(FOX_HEAD_DIM):
            pq[h * FOX_HEAD_DIM + d, base + d] = 1.0
            pk[h * FOX_HEAD_DIM + d, base + d] = 1.0
            pv[h * FOX_HEAD_DIM + d, base + d] = 1.0
        for piece in range(3):
            pk[FOX_DIM + piece * LANES + h, base + FOX_HEAD_DIM + piece] = 1.0
            pq[FOX_DIM + piece * LANES + h, base + FOX_HEAD_DIM + 3 + piece] = 1.0
            oneq[0, base + FOX_HEAD_DIM + piece] = 1.0
            onek[0, base + FOX_HEAD_DIM + 3 + piece] = 1.0
        onev[0, base + FOX_HEAD_DIM] = 1.0
    gmat = np.kron(np.eye(FOX_HEADS, dtype=np.float32),
                   np.full((FOX_HEAD_DIM, FOX_HEAD_DIM), 1.0 / FOX_HEAD_DIM, np.float32))
    return pq, pk, pv, oneq, onek, onev, gmat


def _inproj_weights(norm_mix_g, w_in, fox_b_f, fox_q_norm_g, fox_k_norm_g, gla_w_gup, gla_b_g):
    o = 0
    cols = {}
    for name, width in (("fq", FOX_DIM), ("fk", FOX_DIM), ("fv", FOX_DIM), ("ff", FOX_HEADS),
                        ("gq", GLA_KDIM), ("gk", GLA_KDIM), ("gv", GLA_VDIM), ("gr", GLA_VDIM),
                        ("glr", GLA_GATE_RANK)):
        cols[name] = w_in[:, o:o + width]
        o += width
    pad = jnp.zeros((D_MODEL, SMALL_COLS - FOX_HEADS - GLA_GATE_RANK), w_in.dtype)
    w = jnp.concatenate([cols["fq"], cols["fk"], cols["fv"], cols["gq"], cols["gk"], cols["gv"],
                         cols["gr"], cols["ff"], cols["glr"], pad], axis=1).astype(BF16)
    wg = jnp.zeros((SMALL_COLS, GLA_KDIM), F32).at[FOX_HEADS:FOX_HEADS + GLA_GATE_RANK].set(gla_w_gup)
    bf = jnp.zeros((1, SMALL_COLS), F32).at[0, :FOX_HEADS].set(fox_b_f)
    qg = jnp.tile(fox_q_norm_g, FOX_HEADS)[None, :] * (FOX_HEAD_DIM ** -0.5 * LOG2E)
    kg = jnp.tile(fox_k_norm_g, FOX_HEADS)[None, :]
    return dict(gmix=norm_mix_g[None, :], w=w, qg=qg, kg=kg, bf=bf, wg=wg.astype(BF16),
                bg=gla_b_g[None, :])


def _inproj(x, wts, *, t_real, prompt, seq_rows=0):
    nb, rows, _ = x.shape
    tm = ROW_TILE
    nt = rows // tm
    pq, pk, pv, oneq, onek, onev, gmat = _placement_constants()
    row_spec = lambda width: pl.BlockSpec((1, tm, width), lambda b, i: (b, i, 0))
    in_specs = [row_spec(D_MODEL), _const_spec((1, D_MODEL)), _const_spec((D_MODEL, _W_COLS)),
                _const_spec((FOX_DIM, FOX_DIM)), _const_spec((1, FOX_DIM)), _const_spec((1, FOX_DIM)),
                _const_spec((1, SMALL_COLS)), _const_spec((SMALL_COLS, GLA_KDIM)),
                _const_spec((1, GLA_KDIM))]
    args = [x, wts["gmix"], wts["w"], jnp.asarray(gmat, BF16), wts["qg"], wts["kg"], wts["bf"],
            wts["wg"], wts["bg"]]
    gla_shapes = [jax.ShapeDtypeStruct((nb, rows, GLA_KDIM), F32),
                  jax.ShapeDtypeStruct((nb, rows, GLA_KDIM), F32),
                  jax.ShapeDtypeStruct((nb, rows, GLA_VDIM), F32),
                  jax.ShapeDtypeStruct((nb, rows, GLA_VDIM), F32),
                  jax.ShapeDtypeStruct((nb, rows, GLA_KDIM), F32)]
    gla_specs = [row_spec(GLA_KDIM), row_spec(GLA_KDIM), row_spec(GLA_VDIM), row_spec(GLA_VDIM),
                 row_spec(GLA_KDIM)]
    scratch = []
    if prompt:
        n_cat = FOX_DIM + 3 * LANES
        tri = np.tril(np.ones((tm, tm), np.float32))
        in_specs += [_const_spec((tm, tm)), _const_spec((n_cat, AUG_DIM)), _const_spec((n_cat, AUG_DIM)),
                     _const_spec((FOX_DIM, AUG_DIM)), _const_spec((1, AUG_DIM)), _const_spec((1, AUG_DIM)),
                     _const_spec((1, AUG_DIM))]
        args += [jnp.asarray(tri, BF16), jnp.asarray(pq, BF16), jnp.asarray(pk, BF16),
                 jnp.asarray(pv, BF16), jnp.asarray(oneq), jnp.asarray(onek), jnp.asarray(onev)]
        out_shape = [jax.ShapeDtypeStruct((nb, rows, AUG_DIM), BF16)] * 3
        out_specs = [row_spec(AUG_DIM)] * 3
        col_spec = lambda height: pl.BlockSpec((1, height, tm), lambda b, i: (b, 0, i))
        out_shape += [jax.ShapeDtypeStruct((nb, FOX_DIM, t_real), F32),
                      jax.ShapeDtypeStruct((nb, FOX_DIM, t_real), F32),
                      jax.ShapeDtypeStruct((nb, FOX_HEADS, t_real), F32)]
        out_specs += [col_spec(FOX_DIM), col_spec(FOX_DIM), col_spec(FOX_HEADS)]
        scratch = [pltpu.VMEM((1, SMALL_COLS), F32)]
    else:
        out_shape = [jax.ShapeDtypeStruct((nb, rows, FOX_DIM), BF16),
                     jax.ShapeDtypeStruct((nb, rows, FOX_DIM), F32),
                     jax.ShapeDtypeStruct((nb, rows, FOX_DIM), F32),
                     jax.ShapeDtypeStruct((nb, rows, FOX_HEADS), F32)]
        out_specs = [row_spec(FOX_DIM), row_spec(FOX_DIM), row_spec(FOX_DIM), row_spec(FOX_HEADS)]
    out_shape += gla_shapes
    out_specs += gla_specs
    return pl.pallas_call(
        functools.partial(_inproj_kernel, tm=tm, t_real=t_real, seq_rows=seq_rows, prompt=prompt),
        grid=(nb, nt),
        in_specs=in_specs,
        out_specs=out_specs,
        out_shape=out_shape,
        scratch_shapes=scratch,
        compiler_params=pltpu.CompilerParams(dimension_semantics=("arbitrary", "arbitrary"),
                                             vmem_limit_bytes=VMEM_LIMIT),
        name="inproj_prompt" if prompt else "inproj_decode",
    )(*args)


def _fox_prompt_kernel(q_ref, k_ref, v_ref, o_ref, m_ref, acc_ref, *, tile):
    i = pl.program_id(2)
    q = q_ref[0]
    m_ref[...] = jnp.full_like(m_ref, -jnp.inf)
    acc_ref[...] = jnp.zeros_like(acc_ref)

    def step(j, causal):
        start = pl.multiple_of(j * tile, tile)
        k = k_ref[0, pl.ds(start, tile), :]
        v = v_ref[0, pl.ds(start, tile), :]
        s = lax.dot_general(q, k, (((1,), (1,)), ((), ())), preferred_element_type=F32)
        if causal:
            r = lax.broadcasted_iota(jnp.int32, (tile, tile), 0)
            c = lax.broadcasted_iota(jnp.int32, (tile, tile), 1)
            s = jnp.where(c <= r, s, -jnp.inf)
        m_old = m_ref[...]
        m_new = jnp.maximum(m_old, jnp.max(s, axis=-1, keepdims=True))
        p = jnp.exp2(s - m_new)
        acc_ref[...] = acc_ref[...] * jnp.exp2(m_old - m_new) + jnp.dot(
            p.astype(BF16), v, preferred_element_type=F32)
        m_ref[...] = m_new

    def body(j, carry):
        step(j, False)
        return carry

    lax.fori_loop(0, i, body, 0)
    step(i, True)
    acc = acc_ref[...]
    o_ref[0] = (acc / acc[:, FOX_HEAD_DIM:FOX_HEAD_DIM + 1]).astype(BF16)


def _fox_prompt(qa, ka, va):
    nb, rows, _ = qa.shape
    tile = ROW_TILE
    nq = rows // tile
    return pl.pallas_call(
        functools.partial(_fox_prompt_kernel, tile=tile),
        grid=(nb, FOX_HEADS, nq),
        in_specs=[pl.BlockSpec((1, tile, LANES), lambda b, h, i: (b, i, h)),
                  pl.BlockSpec((1, rows, LANES), lambda b, h, i: (b, 0, h)),
                  pl.BlockSpec((1, rows, LANES), lambda b, h, i: (b, 0, h))],
        out_specs=pl.BlockSpec((1, tile, LANES), lambda b, h, i: (b, i, h)),
        out_shape=jax.ShapeDtypeStruct((nb, rows, AUG_DIM), BF16),
        scratch_shapes=[pltpu.VMEM((tile, 1), F32), pltpu.VMEM((tile, LANES), F32)],
        compiler_params=pltpu.CompilerParams(
            dimension_semantics=("arbitrary", "arbitrary", "arbitrary"), vmem_limit_bytes=VMEM_LIMIT),
        name="fox_prompt",
    )(qa, ka, va)


def _fox_decode_kernel(pt_ref, qbd_ref, kn_ref, vn_ref, lfn_ref, usuf_ref, *rest, n_pages, t_new):
    del pt_ref
    p = n_pages
    k_refs, v_refs, lf_refs = rest[0:p], rest[p:2 * p], rest[2 * p:3 * p]
    o_ref, m_ref, l_ref, acc_ref, carry_ref, qoff_ref = rest[3 * p:]
    j = pl.program_id(1)
    nq = DECODE_ROWS * FOX_HEADS
    qbd = qbd_ref[0]

    def suffix_sums(lf, token_minor):
        n = lf.shape[1] if token_minor else lf.shape[0]
        u = usuf_ref[0:n, :]
        u = jnp.concatenate([u[:, 0:n], u[:, PAGE_SIZE:PAGE_SIZE + LANES]], axis=1)
        if token_minor:
            r = jnp.dot(jnp.concatenate(_split3(lf), axis=0), u, preferred_element_type=F32)
        else:
            r = lax.dot_general(jnp.concatenate(_split3(lf), axis=1), u, (((0,), (0,)), ((), ())),
                                preferred_element_type=F32)
        r = r[0:FOX_HEADS] + r[FOX_HEADS:2 * FOX_HEADS] + r[2 * FOX_HEADS:3 * FOX_HEADS]
        return r[:, 0:n], r[:, n:n + LANES]

    def attend(k, v, bias_t, causal, token_minor):
        k = k.astype(BF16)
        v = v.astype(BF16)
        if token_minor:
            n = k.shape[1]
            s = jnp.dot(qbd, k, preferred_element_type=F32)
        else:
            n = k.shape[0]
            s = lax.dot_general(qbd, k, (((1,), (1,)), ((), ())), preferred_element_type=F32)
        s = s + jnp.concatenate([bias_t] * DECODE_ROWS, axis=0) - qoff_ref[...]
        if causal:
            r = lax.broadcasted_iota(jnp.int32, (nq, n), 0)
            c = lax.broadcasted_iota(jnp.int32, (nq, n), 1)
            s = jnp.where(c * FOX_HEADS <= r, s, -jnp.inf)
        m_old = m_ref[...]
        m_new = jnp.maximum(m_old, jnp.max(s, axis=-1, keepdims=True))
        alpha = jnp.exp2(m_old - m_new)
        pr = jnp.exp2(s - m_new)
        l_ref[...] = l_ref[...] * alpha + jnp.sum(pr, axis=-1, keepdims=True)
        pv_dims = (((1,), (1,)), ((), ())) if token_minor else (((1,), (0,)), ((), ()))
        acc_ref[...] = acc_ref[...] * alpha + lax.dot_general(pr.astype(BF16), v, pv_dims,
                                                              preferred_element_type=F32)
        m_ref[...] = m_new

    @pl.when(j == 0)
    def _():
        m_ref[...] = jnp.full_like(m_ref, -jnp.inf)
        l_ref[...] = jnp.zeros_like(l_ref)
        acc_ref[...] = jnp.zeros_like(acc_ref)
        rowi = lax.broadcasted_iota(jnp.int32, (DECODE_ROWS, FOX_HEADS), 0)
        lf = jnp.where(rowi < t_new, lfn_ref[0], 0.0) * LOG2E
        r_t, tot = suffix_sums(lf, False)
        rt_rows = jnp.concatenate([r_t] * DECODE_ROWS, axis=0)
        r = lax.broadcasted_iota(jnp.int32, (nq, DECODE_ROWS), 0)
        c = lax.broadcasted_iota(jnp.int32, (nq, DECODE_ROWS), 1)
        own_t = jnp.logical_and(c * FOX_HEADS <= r, r < (c + 1) * FOX_HEADS)
        qoff_ref[...] = jnp.sum(jnp.where(own_t, rt_rows, 0.0), axis=-1, keepdims=True)
        attend(kn_ref[0], vn_ref[0], r_t, True, False)
        carry_ref[...] = tot

    for idx in range(p):
        r_t, tot = suffix_sums(lf_refs[idx][...] * LOG2E, True)
        attend(k_refs[idx][...], v_refs[idx][...], r_t + carry_ref[...], False, True)
        carry_ref[...] = carry_ref[...] + tot

    @pl.when(j == pl.num_programs(1) - 1)
    def _():
        o = acc_ref[...] / l_ref[...]
        r = lax.broadcasted_iota(jnp.int32, (nq, FOX_DIM), 0)
        c = lax.broadcasted_iota(jnp.int32, (nq, FOX_DIM), 1)
        own = (c // FOX_HEAD_DIM) == (r % FOX_HEADS)
        o = jnp.where(own, o, 0.0).reshape(DECODE_ROWS, FOX_HEADS, FOX_DIM)
        o_ref[0] = jnp.sum(o, axis=1).astype(BF16)


def _fox_decode(page_table, qbd, k_new, v_new, lf_new, cache_k, cache_v, cache_logf, *, t_new):
    n_seq, n_pages_seq = page_table.shape
    p = DECODE_PAGES_PER_STEP
    n_steps = n_pages_seq // p
    nq = DECODE_ROWS * FOX_HEADS
    n_pool = cache_k.shape[0]
    ck = jnp.transpose(cache_k, (0, 2, 3, 1)).reshape(n_pool, FOX_DIM, PAGE_SIZE)
    cv = jnp.transpose(cache_v, (0, 2, 3, 1)).reshape(n_pool, FOX_DIM, PAGE_SIZE)
    clf = jnp.transpose(cache_logf, (0, 2, 1))
    usuf = np.concatenate([np.tril(np.ones((PAGE_SIZE, PAGE_SIZE), np.float32), -1),
                           np.ones((PAGE_SIZE, LANES), np.float32)], axis=1)

    def page_map(idx):
        return lambda n, j, pt: (pt[n * n_pages_seq + n_pages_seq - 1 - (j * p + idx)], 0, 0)

    seq_spec = lambda r, w: pl.BlockSpec((1, r, w), lambda n, j, pt: (n, 0, 0))
    in_specs = [seq_spec(nq, FOX_DIM), seq_spec(DECODE_ROWS, FOX_DIM), seq_spec(DECODE_ROWS, FOX_DIM),
                seq_spec(DECODE_ROWS, FOX_HEADS),
                pl.BlockSpec((PAGE_SIZE, PAGE_SIZE + LANES), lambda n, j, pt: (0, 0))]
    in_specs += [pl.BlockSpec((None, FOX_DIM, PAGE_SIZE), page_map(idx)) for idx in range(p)]
    in_specs += [pl.BlockSpec((None, FOX_DIM, PAGE_SIZE), page_map(idx)) for idx in range(p)]
    in_specs += [pl.BlockSpec((None, FOX_HEADS, PAGE_SIZE), page_map(idx)) for idx in range(p)]
    grid_spec = pltpu.PrefetchScalarGridSpec(
        num_scalar_prefetch=1,
        grid=(n_seq, n_steps),
        in_specs=in_specs,
        out_specs=pl.BlockSpec((1, DECODE_ROWS, FOX_DIM), lambda n, j, pt: (n, 0, 0)),
        scratch_shapes=[pltpu.VMEM((nq, 1), F32), pltpu.VMEM((nq, 1), F32), pltpu.VMEM((nq, FOX_DIM), F32),
                        pltpu.VMEM((FOX_HEADS, LANES), F32), pltpu.VMEM((nq, 1), F32)],
    )
    return pl.pallas_call(
        functools.partial(_fox_decode_kernel, n_pages=p, t_new=t_new),
        grid_spec=grid_spec,
        out_shape=jax.ShapeDtypeStruct((n_seq, DECODE_ROWS, FOX_DIM), BF16),
        compiler_params=pltpu.CompilerParams(dimension_semantics=("arbitrary", "arbitrary"),
                                             vmem_limit_bytes=VMEM_LIMIT),
        name="fox_decode",
    )(page_table.reshape(-1), qbd, k_new, v_new, lf_new, jnp.asarray(usuf, BF16),
      *([ck] * p), *([cv] * p), *([clf] * p))


def _gla_kernel(gq_ref, gk_ref, gv_ref, gr_ref, gg_ref, s0_ref, gn_ref, tri_ref,
                o_ref, sfin_ref, st_ref, *, tm, chunk):
    i = pl.program_id(1)
    n_pairs = GLA_HEADS // 2
    pair = 2 * GLA_KEY_DIM

    @pl.when(i == 0)
    def _():
        for hp in range(n_pairs):
            st_ref[hp] = s0_ref[0, 2 * hp:2 * hp + 2].reshape(pair, GLA_VAL_DIM).T

    lane = lax.broadcasted_iota(jnp.int32, (chunk, pair), 1)
    rr = lax.broadcasted_iota(jnp.int32, (chunk, chunk), 0)
    cc = lax.broadcasted_iota(jnp.int32, (chunk, chunk), 1)
    causal = cc <= rr
    tri = tri_ref[...]
    gn = gn_ref[...]

    def chunk_body(c, carry):
        r0 = pl.multiple_of(c * chunk, chunk)
        rows = pl.ds(r0, chunk)
        for hp in range(n_pairs):
            ks = slice(hp * pair, (hp + 1) * pair)
            g = gg_ref[0, rows, ks]
            q = gq_ref[0, rows, ks]
            k = gk_ref[0, rows, ks]
            cb = jnp.dot(tri, jnp.concatenate(_split3(g), axis=1), preferred_element_type=F32)
            bc = cb[:, 0:pair] + cb[:, pair:2 * pair] + cb[:, 2 * pair:3 * pair]
            b_last = bc[chunk - 1:chunk, :]
            e_pos = jnp.exp(bc)
            qt = q * e_pos
            kt = (k * jnp.exp(-bc)).astype(BF16)
            kh = k * jnp.exp(b_last - bc)
            st = st_ref[hp]
            st_bf = st.astype(BF16)
            st_new = st * jnp.exp(b_last)
            for hh in range(2):
                h = 2 * hp + hh
                own = (lane // GLA_KEY_DIM) == hh
                vs = slice(h * GLA_VAL_DIM, (h + 1) * GLA_VAL_DIM)
                v = gv_ref[0, rows, vs].astype(BF16)
                qm = jnp.where(own, qt, 0.0).astype(BF16)
                a = lax.dot_general(qm, kt, (((1,), (1,)), ((), ())), preferred_element_type=F32)
                a = jnp.where(causal, a, 0.0).astype(BF16)
                o = jnp.dot(a, v, preferred_element_type=F32)
                o = o + lax.dot_general(qm, st_bf, (((1,), (1,)), ((), ())), preferred_element_type=F32)
                khm = jnp.where(own, kh, 0.0).astype(BF16)
                st_new = st_new + lax.dot_general(v, khm, (((0,), (0,)), ((), ())),
                                                  preferred_element_type=F32)
                on = o * lax.rsqrt(jnp.mean(o * o, axis=-1, keepdims=True) + EPS) * gn
                gate = gr_ref[0, rows, vs]
                gate = gate * (1.0 / (1.0 + jnp.exp(-gate)))
                o_ref[0, rows, vs] = (on * gate).astype(BF16)
            st_ref[hp] = st_new
        return carry

    lax.fori_loop(0, tm // chunk, chunk_body, 0)

    @pl.when(i == pl.num_programs(1) - 1)
    def _():
        for hp in range(n_pairs):
            sfin_ref[0, 2 * hp:2 * hp + 2] = st_ref[hp].T.reshape(2, GLA_KEY_DIM, GLA_VAL_DIM)


def _gla(gq, gk, gv, gr, gg, s0, gla_out_norm_g, *, tm, chunk):
    nb, rows, _ = gq.shape
    nt = rows // tm
    tri = np.tril(np.ones((chunk, chunk), np.float32))
    row_spec = lambda width: pl.BlockSpec((1, tm, width), lambda b, i: (b, i, 0))
    state_spec = pl.BlockSpec((1, GLA_HEADS, GLA_KEY_DIM, GLA_VAL_DIM), lambda b, i: (b, 0, 0, 0))
    return pl.pallas_call(
        functools.partial(_gla_kernel, tm=tm, chunk=chunk),
        grid=(nb, nt),
        in_specs=[row_spec(GLA_KDIM), row_spec(GLA_KDIM), row_spec(GLA_VDIM), row_spec(GLA_VDIM),
                  row_spec(GLA_KDIM), state_spec,
                  pl.BlockSpec((1, GLA_VAL_DIM), lambda b, i: (0, 0)),
                  pl.BlockSpec((chunk, chunk), lambda b, i: (0, 0))],
        out_specs=[row_spec(GLA_VDIM), state_spec],
        out_shape=[jax.ShapeDtypeStruct((nb, rows, GLA_VDIM), BF16),
                   jax.ShapeDtypeStruct((nb, GLA_HEADS, GLA_KEY_DIM, GLA_VAL_DIM), F32)],
        scratch_shapes=[pltpu.VMEM((GLA_HEADS // 2, GLA_VAL_DIM, 2 * GLA_KEY_DIM), F32)],
        compiler_params=pltpu.CompilerParams(dimension_semantics=("arbitrary", "arbitrary"),
                                             vmem_limit_bytes=VMEM_LIMIT),
        name="gla",
    )(gq, gk, gv, gr, gg, s0, gla_out_norm_g[None, :], jnp.asarray(tri, BF16))


def _out_mlp_kernel(x_ref, of_ref, og_ref, wof_ref, wog_ref, gm_ref, wup_ref, wdn_ref, y_ref, *, ff_chunk):
    x = x_ref[0]
    h = (x + jnp.dot(of_ref[0], wof_ref[...], preferred_element_type=F32)
         + jnp.dot(og_ref[0], wog_ref[...], preferred_element_type=F32))
    ms = jnp.mean(h * h, axis=-1, keepdims=True)
    hn = (h * lax.rsqrt(ms + EPS) * gm_ref[...]).astype(BF16)
    y_ref[0] = h
    for c in range(D_FF // ff_chunk):
        cs = slice(c * ff_chunk, (c + 1) * ff_chunk)
        u = jnp.maximum(jnp.dot(hn, wup_ref[:, cs], preferred_element_type=F32), 0.0)
        y_ref[0] += jnp.dot((u * u).astype(BF16), wdn_ref[cs, :], preferred_element_type=F32)


def _out_mlp(x, o_fox, o_gla, wof, wog, norm_mlp_g, w_up, w_down, *, fox_row_offset=0):
    nb, rows, _ = x.shape
    tm = ROW_TILE
    nt = rows // tm
    kf = o_fox.shape[-1]
    if fox_row_offset:
        assert fox_row_offset % 16 == 0, "row offset must keep bf16 sublane tiles aligned"
        shifted = lambda b, i: (b, pl.multiple_of(i * tm + fox_row_offset, 16), 0)
        of_spec = pl.BlockSpec((pl.Element(1), pl.Element(tm), pl.Element(kf)), shifted)
        og_spec = pl.BlockSpec((pl.Element(1), pl.Element(tm), pl.Element(GLA_VDIM)), shifted)
    else:
        of_spec = pl.BlockSpec((1, tm, kf), lambda b, i: (b, i, 0))
        og_spec = pl.BlockSpec((1, tm, GLA_VDIM), lambda b, i: (b, i, 0))
    return pl.pallas_call(
        functools.partial(_out_mlp_kernel, ff_chunk=D_MODEL),
        grid=(nb, nt),
        in_specs=[pl.BlockSpec((1, tm, D_MODEL), lambda b, i: (b, i, 0)), of_spec, og_spec,
                  _const_spec((kf, D_MODEL)), _const_spec((GLA_VDIM, D_MODEL)), _const_spec((1, D_MODEL)),
                  _const_spec((D_MODEL, D_FF)), _const_spec((D_FF, D_MODEL))],
        out_specs=pl.BlockSpec((1, tm, D_MODEL), lambda b, i: (b, i, 0)),
        out_shape=jax.ShapeDtypeStruct((nb, rows, D_MODEL), F32),
        compiler_params=pltpu.CompilerParams(dimension_semantics=("arbitrary", "arbitrary"),
                                             vmem_limit_bytes=VMEM_LIMIT),
        name="out_mlp",
    )(x, o_fox, o_gla, wof, wog, norm_mlp_g[None, :], w_up, w_down)


def kernel(x_prompt, x_sample, cache_k, cache_v, cache_logf, state_gla, page_table, meta_tokens, norm_mix_g,
           w_in, fox_b_f, fox_q_norm_g, fox_k_norm_g, gla_w_gup, gla_b_g, gla_out_norm_g, w_out, norm_mlp_g,
           w_up, w_down):
    depth = w_in.shape[0]
    assert depth == 1, "single-layer trunk"
    nb, seq, _ = x_prompt.shape
    n_seq, dec_seq, _ = x_sample.shape
    t_real = N_META + seq
    t_pad = -(-t_real // ROW_TILE) * ROW_TILE

    wts = _inproj_weights(norm_mix_g[0], w_in[0], fox_b_f[0], fox_q_norm_g[0], fox_k_norm_g[0],
                          gla_w_gup[0], gla_b_g[0])
    w_out_l = w_out[0]
    wof_aug = jnp.zeros((FOX_HEADS, LANES, D_MODEL), F32).at[:, :FOX_HEAD_DIM].set(
        w_out_l[:FOX_DIM].reshape(FOX_HEADS, FOX_HEAD_DIM, D_MODEL)).reshape(AUG_DIM, D_MODEL).astype(BF16)
    wof = w_out_l[:FOX_DIM].astype(BF16)
    wog = w_out_l[FOX_DIM:].astype(BF16)
    w_up_b = w_up[0].astype(BF16)
    w_dn_b = w_down[0].astype(BF16)

    meta = jnp.broadcast_to(meta_tokens[None].astype(x_prompt.dtype), (nb, N_META, D_MODEL))
    xp = jnp.concatenate([meta, x_prompt, jnp.zeros((nb, t_pad - t_real, D_MODEL), x_prompt.dtype)], axis=1)
    qa, ka, va, k_p, v_p, lf_p, gq, gk, gv, gr, gg = _inproj(xp, wts, t_real=t_real, prompt=True)
    o_fox = _fox_prompt(qa, ka, va)
    s0 = jnp.zeros((nb, GLA_HEADS, GLA_KEY_DIM, GLA_VAL_DIM), F32)
    o_gla, s_p = _gla(gq, gk, gv, gr, gg, s0, gla_out_norm_g[0], tm=ROW_TILE, chunk=GLA_CHUNK)
    y_p = _out_mlp(x_prompt, o_fox, o_gla, wof_aug, wog, norm_mlp_g[0], w_up_b, w_dn_b,
                   fox_row_offset=N_META)

    xs = jnp.pad(x_sample, ((0, 0), (0, DECODE_ROWS - dec_seq), (0, 0))).reshape(1, n_seq * DECODE_ROWS, D_MODEL)
    qc, k_s, v_s, lf_s, hq, hk, hv, hr, hg = _inproj(xs, wts, t_real=dec_seq, prompt=False,
                                                      seq_rows=DECODE_ROWS)
    per_seq = lambda a: a.reshape(n_seq, DECODE_ROWS, a.shape[-1])
    head_of_lane = jnp.arange(FOX_DIM) // FOX_HEAD_DIM
    qmask = (head_of_lane[None, :] == jnp.arange(FOX_HEADS)[:, None]).astype(BF16)
    qbd = (per_seq(qc)[:, :, None, :] * qmask[None, None]).reshape(n_seq, DECODE_ROWS * FOX_HEADS, FOX_DIM)
    o_fox_s = _fox_decode(page_table, qbd, per_seq(k_s), per_seq(v_s), per_seq(lf_s),
                          cache_k[0], cache_v[0], cache_logf[0], t_new=dec_seq)
    o_gla_s, s_s = _gla(per_seq(hq), per_seq(hk), per_seq(hv), per_seq(hr), per_seq(hg),
                        state_gla[0].astype(F32), gla_out_norm_g[0], tm=DECODE_ROWS, chunk=DECODE_ROWS)
    y_s = _out_mlp(xs, o_fox_s.reshape(1, n_seq * DECODE_ROWS, FOX_DIM),
                   o_gla_s.reshape(1, n_seq * DECODE_ROWS, GLA_VDIM), wof, wog, norm_mlp_g[0], w_up_b, w_dn_b)

    real = lambda a: per_seq(a[0])[:, :dec_seq]
    return (y_p,
            real(y_s),
            jnp.transpose(k_p.reshape(1, nb, FOX_HEADS, FOX_HEAD_DIM, t_real), (0, 1, 4, 2, 3)),
            jnp.transpose(v_p.reshape(1, nb, FOX_HEADS, FOX_HEAD_DIM, t_real), (0, 1, 4, 2, 3)),
            jnp.transpose(lf_p, (0, 2, 1))[None],
            s_p[None],
            real(k_s).reshape(1, n_seq, dec_seq, FOX_HEADS, FOX_HEAD_DIM),
            real(v_s).reshape(1, n_seq, dec_seq, FOX_HEADS, FOX_HEAD_DIM),
            real(lf_s)[None],
            s_s[None])
```

```python
import functools

import jax
import jax.numpy as jnp
import numpy as np
from jax import lax
from jax.experimental import pallas as pl
from jax.experimental.pallas import tpu as pltpu

D_MODEL = 1024
N_META = 16
PAGE_SIZE = 128
FOX_HEADS = 8
FOX_HEAD_DIM = 64
FOX_DIM = FOX_HEADS * FOX_HEAD_DIM
GLA_HEADS = 4
GLA_KEY_DIM = 64
GLA_VAL_DIM = 128
GLA_KDIM = GLA_HEADS * GLA_KEY_DIM
GLA_VDIM = GLA_HEADS * GLA_VAL_DIM
GLA_GATE_RANK = 16
GLA_GATE_TEMP = 16.0
GLA_CHUNK = 64
D_FF = 4 * D_MODEL
EPS = 1e-6
LOG2E = 1.4426950408889634

LANES = 128
AUG_DIM = FOX_HEADS * LANES
AUG_ROWS = 16
V_ROWS = FOX_HEAD_DIM + AUG_ROWS
SMALL_COLS = LANES
ROW_TILE = 512
PROMPT_TILES_PER_TRIP = 4
DECODE_ROWS = 8
DECODE_PAGES_PER_STEP = 16
GLA_DECODE_SEQS_PER_STEP = 4
VMEM_LIMIT = 56 * 1024 * 1024

F32 = jnp.float32
BF16 = jnp.bfloat16

_C_FQ, _C_FK, _C_FV = 0, FOX_DIM, 2 * FOX_DIM
_C_GQ = 3 * FOX_DIM
_C_GK = _C_GQ + GLA_KDIM
_C_GV = _C_GK + GLA_KDIM
_C_GR = _C_GV + GLA_VDIM
_C_SM = _C_GR + GLA_VDIM
_W_COLS = _C_SM + SMALL_COLS


def _log_sigmoid(x):
    return jnp.minimum(x, 0.0) - jnp.log1p(jnp.exp(-jnp.abs(x)))


def _split3(x):
    p1 = x.astype(BF16)
    r = x - p1.astype(F32)
    p2 = r.astype(BF16)
    r = r - p2.astype(F32)
    return p1, p2, r.astype(BF16)


def _const_spec(shape):
    return pl.BlockSpec(shape, lambda *_: (0,) * len(shape), pipeline_mode=pl.Buffered(1))


def _inproj_kernel(x_ref, gmix_ref, w_ref, gmat_ref, qg_ref, kg_ref, bf_ref, wg_ref, bg_ref, *rest,
                   tm, t_real, seq_rows, prompt):
    if prompt:
        (tri_ref, pk_ref, onek_ref, selq_ref, oneq_ref, onev_ref,
         qt_ref, ka_ref, vt_ref, ko_ref, vo_ref, lf_ref,
         gq_ref, gk_ref, gv_ref, gr_ref, gg_ref, carry_ref) = rest
    else:
        (qc_ref, ko_ref, vo_ref, lf_ref, gq_ref, gk_ref, gv_ref, gr_ref, gg_ref) = rest
    i = pl.program_id(1)

    x = x_ref[0]
    ms = jnp.mean(x * x, axis=-1, keepdims=True)
    xn = (x * lax.rsqrt(ms + EPS) * gmix_ref[...]).astype(BF16)
    z = jnp.dot(xn, w_ref[...], preferred_element_type=F32)

    def head_norm(zz, g_row):
        msq = jnp.dot((zz * zz).astype(BF16), gmat_ref[...], preferred_element_type=F32)
        return zz * lax.rsqrt(msq + EPS) * g_row

    qn = head_norm(z[:, _C_FQ:_C_FQ + FOX_DIM], qg_ref[...])
    kn = head_norm(z[:, _C_FK:_C_FK + FOX_DIM], kg_ref[...])
    zv = z[:, _C_FV:_C_FV + FOX_DIM]
    zs = z[:, _C_SM:_C_SM + SMALL_COLS]
    lf = _log_sigmoid(zs + bf_ref[...])
    if prompt:
        ko_ref[0] = kn.T
        zv_t = zv.T
        vo_ref[0] = zv_t
        lf_ref[0] = lf.T[0:FOX_HEADS, :]
    else:
        ko_ref[0] = kn
        vo_ref[0] = zv
        lf_ref[0] = lf[:, 0:FOX_HEADS]

    row = i * tm + lax.broadcasted_iota(jnp.int32, (tm, 1), 0)
    if prompt:
        valid = row < t_real
    else:
        valid = (row % seq_rows) < t_real

    gq_ref[0] = z[:, _C_GQ:_C_GQ + GLA_KDIM] * (GLA_KEY_DIM ** -0.5)
    gk_ref[0] = z[:, _C_GK:_C_GK + GLA_KDIM]
    gv_ref[0] = z[:, _C_GV:_C_GV + GLA_VDIM]
    gr_ref[0] = z[:, _C_GR:_C_GR + GLA_VDIM]
    xg = jnp.dot(zs.astype(BF16), wg_ref[...], preferred_element_type=F32) + bg_ref[...]
    gg_ref[0] = jnp.where(valid, _log_sigmoid(xg) * (1.0 / GLA_GATE_TEMP), 0.0)

    if not prompt:
        qc_ref[0] = qn.astype(BF16)
        return

    @pl.when(i == 0)
    def _():
        carry_ref[...] = jnp.zeros_like(carry_ref)

    lane = lax.broadcasted_iota(jnp.int32, (tm, SMALL_COLS), 1)
    lfm = jnp.where(lane < FOX_HEADS, lf, 0.0)
    c = jnp.dot(tri_ref[...], jnp.concatenate(_split3(lfm), axis=1), preferred_element_type=F32)
    fcum = c[:, 0:LANES] + c[:, LANES:2 * LANES] + c[:, 2 * LANES:3 * LANES] + carry_ref[...]
    carry_ref[...] = fcum[tm - 1:tm, :]
    fs = fcum * LOG2E
    a1, a2, a3 = _split3(fs)

    k_cat = jnp.concatenate([kn.astype(BF16), -a1, -a2, -a3], axis=1)
    ka = jnp.dot(k_cat, pk_ref[...], preferred_element_type=F32) + onek_ref[...]
    ka_ref[0] = ka.astype(BF16)

    qn_t = qn.T
    aug_q = jnp.dot(selq_ref[...], jnp.concatenate(_split3(fs.T), axis=0),
                    preferred_element_type=F32) + oneq_ref[...]
    zero_rows = jnp.zeros((LANES - FOX_HEAD_DIM - AUG_ROWS, tm), BF16)
    for h in range(FOX_HEADS):
        hs = slice(h * FOX_HEAD_DIM, (h + 1) * FOX_HEAD_DIM)
        qt_ref[0, h, 0:FOX_HEAD_DIM, :] = qn_t[hs].astype(BF16)
        qt_ref[0, h, FOX_HEAD_DIM:FOX_HEAD_DIM + AUG_ROWS, :] = aug_q[h * AUG_ROWS:(h + 1) * AUG_ROWS].astype(BF16)
        qt_ref[0, h, FOX_HEAD_DIM + AUG_ROWS:LANES, :] = zero_rows
        vt_ref[0, h, 0:FOX_HEAD_DIM, :] = zv_t[hs].astype(BF16)
        vt_ref[0, h, FOX_HEAD_DIM:FOX_HEAD_DIM + AUG_ROWS, :] = jnp.broadcast_to(
            onev_ref[...], (AUG_ROWS, tm)).astype(BF16)


def _placement_constants():
    n_cat = FOX_DIM + 3 * LANES
    pk = np.zeros((n_cat, AUG_DIM), np.float32)
    onek = np.zeros((1, AUG_DIM), np.float32)
    selq = np.zeros((FOX_HEADS * AUG_ROWS, 3 * LANES), np.float32)
    oneq = np.zeros((FOX_HEADS * AUG_ROWS, 1), np.float32)
    onev = np.zeros((AUG_ROWS, 1), np.float32)
    onev[0, 0] = 1.0
    for h in range(FOX_HEADS):
        base = h * LANES
        for d in range(FOX_HEAD_DIM):
            pk[h * FOX_HEAD_DIM + d, base + d] = 1.0
        for piece in range(3):
            pk[FOX_DIM + piece * LANES + h, base + FOX_HEAD_DIM + piece] = 1.0
            onek[0, base + FOX_HEAD_DIM + 3 + piece] = 1.0
            oneq[h * AUG_ROWS + piece, 0] = 1.0
            selq[h * AUG_ROWS + 3 + piece, piece * LANES + h] = 1.0
    gmat = np.kron(np.eye(FOX_HEADS, dtype=np.float32),
                   np.full((FOX_HEAD_DIM, FOX_HEAD_DIM), 1.0 / FOX_HEAD_DIM, np.float32))
    return pk, onek, selq, oneq, onev, gmat


def _inproj_weights(norm_mix_g, w_in, fox_b_f, fox_q_norm_g, fox_k_norm_g, gla_w_gup, gla_b_g):
    o = 0
    cols = {}
    for name, width in (("fq", FOX_DIM), ("fk", FOX_DIM), ("fv", FOX_DIM), ("ff", FOX_HEADS),
                        ("gq", GLA_KDIM), ("gk", GLA_KDIM), ("gv", GLA_VDIM), ("gr", GLA_VDIM),
                        ("glr", GLA_GATE_RANK)):
        cols[name] = w_in[:, o:o + width]
        o += width
    pad = jnp.zeros((D_MODEL, SMALL_COLS - FOX_HEADS - GLA_GATE_RANK), w_in.dtype)
    w = jnp.concatenate([cols["fq"], cols["fk"], cols["fv"], cols["gq"], cols["gk"], cols["gv"],
                         cols["gr"], cols["ff"], cols["glr"], pad], axis=1).astype(BF16)
    wg = jnp.zeros((SMALL_COLS, GLA_KDIM), F32).at[FOX_HEADS:FOX_HEADS + GLA_GATE_RANK].set(gla_w_gup)
    bf = jnp.zeros((1, SMALL_COLS), F32).at[0, :FOX_HEADS].set(fox_b_f)
    qg = jnp.tile(fox_q_norm_g, FOX_HEADS)[None, :] * (FOX_HEAD_DIM ** -0.5 * LOG2E)
    kg = jnp.tile(fox_k_norm_g, FOX_HEADS)[None, :]
    return dict(gmix=norm_mix_g[None, :], w=w, qg=qg, kg=kg, bf=bf, wg=wg.astype(BF16),
                bg=gla_b_g[None, :])


def _inproj(x, wts, *, t_real, prompt, seq_rows=0):
    nb, rows, _ = x.shape
    tm = ROW_TILE
    nt = rows // tm
    pk, onek, selq, oneq, onev, gmat = _placement_constants()
    row_spec = lambda width: pl.BlockSpec((1, tm, width), lambda b, i: (b, i, 0))
    in_specs = [row_spec(D_MODEL), _const_spec((1, D_MODEL)), _const_spec((D_MODEL, _W_COLS)),
                _const_spec((FOX_DIM, FOX_DIM)), _const_spec((1, FOX_DIM)), _const_spec((1, FOX_DIM)),
                _const_spec((1, SMALL_COLS)), _const_spec((SMALL_COLS, GLA_KDIM)),
                _const_spec((1, GLA_KDIM))]
    args = [x, wts["gmix"], wts["w"], jnp.asarray(gmat, BF16), wts["qg"], wts["kg"], wts["bf"],
            wts["wg"], wts["bg"]]
    gla_shapes = [jax.ShapeDtypeStruct((nb, rows, GLA_KDIM), F32),
                  jax.ShapeDtypeStruct((nb, rows, GLA_KDIM), F32),
                  jax.ShapeDtypeStruct((nb, rows, GLA_VDIM), F32),
                  jax.ShapeDtypeStruct((nb, rows, GLA_VDIM), F32),
                  jax.ShapeDtypeStruct((nb, rows, GLA_KDIM), F32)]
    gla_specs = [row_spec(GLA_KDIM), row_spec(GLA_KDIM), row_spec(GLA_VDIM), row_spec(GLA_VDIM),
                 row_spec(GLA_KDIM)]
    scratch = []
    if prompt:
        n_cat = FOX_DIM + 3 * LANES
        tri = np.tril(np.ones((tm, tm), np.float32))
        in_specs += [_const_spec((tm, tm)), _const_spec((n_cat, AUG_DIM)), _const_spec((1, AUG_DIM)),
                     _const_spec((FOX_HEADS * AUG_ROWS, 3 * LANES)), _const_spec((FOX_HEADS * AUG_ROWS, 1)),
                     _const_spec((AUG_ROWS, 1))]
        args += [jnp.asarray(tri, BF16), jnp.asarray(pk, BF16), jnp.asarray(onek),
                 jnp.asarray(selq, BF16), jnp.asarray(oneq), jnp.asarray(onev)]
        head_spec = lambda height: pl.BlockSpec((1, FOX_HEADS, height, tm), lambda b, i: (b, 0, 0, i))
        out_shape = [jax.ShapeDtypeStruct((nb, FOX_HEADS, LANES, rows), BF16),
                     jax.ShapeDtypeStruct((nb, rows, AUG_DIM), BF16),
                     jax.ShapeDtypeStruct((nb, FOX_HEADS, V_ROWS, rows), BF16)]
        out_specs = [head_spec(LANES), row_spec(AUG_DIM), head_spec(V_ROWS)]
        col_spec = lambda height: pl.BlockSpec((1, height, tm), lambda b, i: (b, 0, i))
        out_shape += [jax.ShapeDtypeStruct((nb, FOX_DIM, t_real), F32),
                      jax.ShapeDtypeStruct((nb, FOX_DIM, t_real), F32),
                      jax.ShapeDtypeStruct((nb, FOX_HEADS, t_real), F32)]
        out_specs += [col_spec(FOX_DIM), col_spec(FOX_DIM), col_spec(FOX_HEADS)]
        scratch = [pltpu.VMEM((1, SMALL_COLS), F32)]
    else:
        out_shape = [jax.ShapeDtypeStruct((nb, rows, FOX_DIM), BF16),
                     jax.ShapeDtypeStruct((nb, rows, FOX_DIM), F32),
                     jax.ShapeDtypeStruct((nb, rows, FOX_DIM), F32),
                     jax.ShapeDtypeStruct((nb, rows, FOX_HEADS), F32)]
        out_specs = [row_spec(FOX_DIM), row_spec(FOX_DIM), row_spec(FOX_DIM), row_spec(FOX_HEADS)]
    out_shape += gla_shapes
    out_specs += gla_specs
    return pl.pallas_call(
        functools.partial(_inproj_kernel, tm=tm, t_real=t_real, seq_rows=seq_rows, prompt=prompt),
        grid=(nb, nt),
        in_specs=in_specs,
        out_specs=out_specs,
        out_shape=out_shape,
        scratch_shapes=scratch,
        compiler_params=pltpu.CompilerParams(dimension_semantics=("arbitrary", "arbitrary"),
                                             vmem_limit_bytes=VMEM_LIMIT),
        name="inproj_prompt" if prompt else "inproj_decode",
    )(*args)


def _fox_prompt_kernel(qt_ref, k_ref, vt_ref, o_ref, sa_ref, sb_ref, ma_ref, mb_ref, m_ref, acc_ref, *, tile):
    i = pl.program_id(2)
    qt = qt_ref[0, 0]
    m_ref[...] = jnp.full_like(m_ref, -jnp.inf)
    acc_ref[...] = jnp.zeros_like(acc_ref)
    bufs = ((sa_ref, ma_ref), (sb_ref, mb_ref))

    def scores(t, parity, causal):
        s_ref, mt_ref = bufs[parity]
        k = k_ref[0, pl.ds(pl.multiple_of(t * tile, tile), tile), :]
        s = jnp.dot(k, qt, preferred_element_type=F32)
        if causal:
            r = lax.broadcasted_iota(jnp.int32, (tile, tile), 0)
            c = lax.broadcasted_iota(jnp.int32, (tile, tile), 1)
            s = jnp.where(r <= c, s, -jnp.inf)
        s_ref[...] = s
        mt_ref[...] = jnp.max(s, axis=0, keepdims=True)

    def accumulate(t, parity):
        s_ref, mt_ref = bufs[parity]
        vt = vt_ref[0, 0, :, pl.ds(pl.multiple_of(t * tile, tile), tile)]
        m_old = m_ref[...]
        m_new = jnp.maximum(m_old, mt_ref[...])
        p = jnp.exp2(s_ref[...] - m_new)
        acc_ref[...] = acc_ref[...] * jnp.exp2(m_old - m_new) + jnp.dot(
            vt, p.astype(BF16), preferred_element_type=F32)
        m_ref[...] = m_new

    def step(t, parity, causal_next):
        scores(t + 1, 1 - parity, causal_next)
        accumulate(t, parity)

    odd = (i % 2) == 1

    @pl.when(i == 0)
    def _():
        scores(0, 0, True)
        accumulate(0, 0)

    @pl.when(i > 0)
    def _():
        scores(0, 0, False)
        unroll = PROMPT_TILES_PER_TRIP
        n_trips = (i - 1) // unroll

        def trip(jj, carry):
            for u in range(unroll):
                step(unroll * jj + u, u % 2, False)
            return carry

        lax.fori_loop(0, n_trips, trip, 0)
        t0 = unroll * n_trips

        @pl.when(i - t0 >= 3)
        def _():
            step(t0, 0, False)
            step(t0 + 1, 1, False)

        @pl.when(odd)
        def _():
            step(i - 1, 0, True)
            accumulate(i, 1)

        @pl.when(jnp.logical_not(odd))
        def _():
            step(i - 2, 0, False)
            step(i - 1, 1, True)
            accumulate(i, 0)

    acc = acc_ref[...]
    o_t = acc[0:FOX_HEAD_DIM] / acc[FOX_HEAD_DIM:FOX_HEAD_DIM + 1]
    o_t = jnp.concatenate([o_t, jnp.zeros((LANES - FOX_HEAD_DIM, tile), F32)], axis=0)
    o_ref[0] = o_t.T.astype(BF16)


def _fox_prompt(qt, ka, vt):
    nb, rows, _ = ka.shape
    tile = ROW_TILE
    nq = rows // tile
    return pl.pallas_call(
        functools.partial(_fox_prompt_kernel, tile=tile),
        grid=(nb, FOX_HEADS, nq),
        in_specs=[pl.BlockSpec((1, 1, LANES, tile), lambda b, h, i: (b, h, 0, i)),
                  pl.BlockSpec((1, rows, LANES), lambda b, h, i: (b, 0, h)),
                  pl.BlockSpec((1, 1, V_ROWS, rows), lambda b, h, i: (b, h, 0, 0))],
        out_specs=pl.BlockSpec((1, tile, LANES), lambda b, h, i: (b, i, h)),
        out_shape=jax.ShapeDtypeStruct((nb, rows, AUG_DIM), BF16),
        scratch_shapes=[pltpu.VMEM((tile, tile), F32), pltpu.VMEM((tile, tile), F32),
                        pltpu.VMEM((1, tile), F32), pltpu.VMEM((1, tile), F32),
                        pltpu.VMEM((1, tile), F32), pltpu.VMEM((V_ROWS, tile), F32)],
        compiler_params=pltpu.CompilerParams(
            dimension_semantics=("arbitrary", "arbitrary", "arbitrary"), vmem_limit_bytes=VMEM_LIMIT),
        name="fox_prompt",
    )(qt, ka, vt)


def _fox_decode_kernel(pt_ref, qbd_ref, kn_ref, vn_ref, lfn_ref, usuf_ref, *rest, n_pages, t_new):
    del pt_ref
    p = n_pages
    k_refs, v_refs, lf_refs = rest[0:p], rest[p:2 * p], rest[2 * p:3 * p]
    o_ref, m_ref, l_ref, acc_ref, carry_ref, qoff_ref = rest[3 * p:]
    j = pl.program_id(1)
    nq = DECODE_ROWS * FOX_HEADS
    qbd = qbd_ref[0]

    def suffix_sums(lf, token_minor):
        n = lf.shape[1] if token_minor else lf.shape[0]
        u = usuf_ref[0:n, :]
        u = jnp.concatenate([u[:, 0:n], u[:, PAGE_SIZE:PAGE_SIZE + LANES]], axis=1)
        if token_minor:
            r = jnp.dot(jnp.concatenate(_split3(lf), axis=0), u, preferred_element_type=F32)
        else:
            r = lax.dot_general(jnp.concatenate(_split3(lf), axis=1), u, (((0,), (0,)), ((), ())),
                                preferred_element_type=F32)
        r = r[0:FOX_HEADS] + r[FOX_HEADS:2 * FOX_HEADS] + r[2 * FOX_HEADS:3 * FOX_HEADS]
        return r[:, 0:n], r[:, n:n + LANES]

    def attend(k, v, bias_t, causal, token_minor):
        k = k.astype(BF16)
        v = v.astype(BF16)
        if token_minor:
            n = k.shape[1]
            s = jnp.dot(qbd, k, preferred_element_type=F32)
        else:
            n = k.shape[0]
            s = lax.dot_general(qbd, k, (((1,), (1,)), ((), ())), preferred_element_type=F32)
        s = s + jnp.concatenate([bias_t] * DECODE_ROWS, axis=0) - qoff_ref[...]
        if causal:
            r = lax.broadcasted_iota(jnp.int32, (nq, n), 0)
            c = lax.broadcasted_iota(jnp.int32, (nq, n), 1)
            s = jnp.where(c * FOX_HEADS <= r, s, -jnp.inf)
        m_old = m_ref[...]
        m_new = jnp.maximum(m_old, jnp.max(s, axis=-1, keepdims=True))
        alpha = jnp.exp2(m_old - m_new)
        pr = jnp.exp2(s - m_new)
        l_ref[...] = l_ref[...] * alpha + jnp.sum(pr, axis=-1, keepdims=True)
        pv_dims = (((1,), (1,)), ((), ())) if token_minor else (((1,), (0,)), ((), ()))
        acc_ref[...] = acc_ref[...] * alpha + lax.dot_general(pr.astype(BF16), v, pv_dims,
                                                              preferred_element_type=F32)
        m_ref[...] = m_new

    @pl.when(j == 0)
    def _():
        m_ref[...] = jnp.full_like(m_ref, -jnp.inf)
        l_ref[...] = jnp.zeros_like(l_ref)
        acc_ref[...] = jnp.zeros_like(acc_ref)
        rowi = lax.broadcasted_iota(jnp.int32, (DECODE_ROWS, FOX_HEADS), 0)
        lf = jnp.where(rowi < t_new, lfn_ref[0], 0.0) * LOG2E
        r_t, tot = suffix_sums(lf, False)
        rt_rows = jnp.concatenate([r_t] * DECODE_ROWS, axis=0)
        r = lax.broadcasted_iota(jnp.int32, (nq, DECODE_ROWS), 0)
        c = lax.broadcasted_iota(jnp.int32, (nq, DECODE_ROWS), 1)
        own_t = jnp.logical_and(c * FOX_HEADS <= r, r < (c + 1) * FOX_HEADS)
        qoff_ref[...] = jnp.sum(jnp.where(own_t, rt_rows, 0.0), axis=-1, keepdims=True)
        attend(kn_ref[0], vn_ref[0], r_t, True, False)
        carry_ref[...] = tot

    n_piece_rows = 3 * FOX_HEADS
    lf_stack = jnp.concatenate(
        [piece for idx in range(p) for piece in _split3(lf_refs[idx][...] * LOG2E)], axis=0)
    r_all = jnp.dot(lf_stack, usuf_ref[...], preferred_element_type=F32)
    carry = carry_ref[...]
    biases = []
    for idx in range(p):
        r = r_all[idx * n_piece_rows:(idx + 1) * n_piece_rows]
        r = r[0:FOX_HEADS] + r[FOX_HEADS:2 * FOX_HEADS] + r[2 * FOX_HEADS:3 * FOX_HEADS]
        biases.append(r[:, 0:PAGE_SIZE] + carry)
        carry = carry + r[:, PAGE_SIZE:PAGE_SIZE + LANES]
    carry_ref[...] = carry
    attend(jnp.concatenate([k_refs[idx][...].astype(BF16) for idx in range(p)], axis=1),
           jnp.concatenate([v_refs[idx][...].astype(BF16) for idx in range(p)], axis=1),
           jnp.concatenate(biases, axis=1), False, True)

    @pl.when(j == pl.num_programs(1) - 1)
    def _():
        o = acc_ref[...] / l_ref[...]
        r = lax.broadcasted_iota(jnp.int32, (nq, FOX_DIM), 0)
        c = lax.broadcasted_iota(jnp.int32, (nq, FOX_DIM), 1)
        own = (c // FOX_HEAD_DIM) == (r % FOX_HEADS)
        o = jnp.where(own, o, 0.0).reshape(DECODE_ROWS, FOX_HEADS, FOX_DIM)
        o_ref[0] = jnp.sum(o, axis=1).astype(BF16)


def _fox_decode(page_table, qbd, k_new, v_new, lf_new, cache_k, cache_v, cache_logf, *, t_new):
    n_seq, n_pages_seq = page_table.shape
    p = DECODE_PAGES_PER_STEP
    n_steps = n_pages_seq // p
    nq = DECODE_ROWS * FOX_HEADS
    n_pool = cache_k.shape[0]
    ck = jnp.transpose(cache_k, (0, 2, 3, 1)).reshape(n_pool, FOX_DIM, PAGE_SIZE)
    cv = jnp.transpose(cache_v, (0, 2, 3, 1)).reshape(n_pool, FOX_DIM, PAGE_SIZE)
    clf = jnp.transpose(cache_logf, (0, 2, 1))
    usuf = np.concatenate([np.tril(np.ones((PAGE_SIZE, PAGE_SIZE), np.float32), -1),
                           np.ones((PAGE_SIZE, LANES), np.float32)], axis=1)

    def page_map(idx):
        return lambda n, j, pt: (pt[n * n_pages_seq + n_pages_seq - 1 - (j * p + idx)], 0, 0)

    seq_spec = lambda r, w: pl.BlockSpec((1, r, w), lambda n, j, pt: (n, 0, 0))
    in_specs = [seq_spec(nq, FOX_DIM), seq_spec(DECODE_ROWS, FOX_DIM), seq_spec(DECODE_ROWS, FOX_DIM),
                seq_spec(DECODE_ROWS, FOX_HEADS),
                pl.BlockSpec((PAGE_SIZE, PAGE_SIZE + LANES), lambda n, j, pt: (0, 0))]
    in_specs += [pl.BlockSpec((None, FOX_DIM, PAGE_SIZE), page_map(idx)) for idx in range(p)]
    in_specs += [pl.BlockSpec((None, FOX_DIM, PAGE_SIZE), page_map(idx)) for idx in range(p)]
    in_specs += [pl.BlockSpec((None, FOX_HEADS, PAGE_SIZE), page_map(idx)) for idx in range(p)]
    grid_spec = pltpu.PrefetchScalarGridSpec(
        num_scalar_prefetch=1,
        grid=(n_seq, n_steps),
        in_specs=in_specs,
        out_specs=pl.BlockSpec((1, DECODE_ROWS, FOX_DIM), lambda n, j, pt: (n, 0, 0)),
        scratch_shapes=[pltpu.VMEM((nq, 1), F32), pltpu.VMEM((nq, 1), F32), pltpu.VMEM((nq, FOX_DIM), F32),
                        pltpu.VMEM((FOX_HEADS, LANES), F32), pltpu.VMEM((nq, 1), F32)],
    )
    return pl.pallas_call(
        functools.partial(_fox_decode_kernel, n_pages=p, t_new=t_new),
        grid_spec=grid_spec,
        out_shape=jax.ShapeDtypeStruct((n_seq, DECODE_ROWS, FOX_DIM), BF16),
        compiler_params=pltpu.CompilerParams(dimension_semantics=("arbitrary", "arbitrary"),
                                             vmem_limit_bytes=VMEM_LIMIT),
        name="fox_decode",
    )(page_table.reshape(-1), qbd, k_new, v_new, lf_new, jnp.asarray(usuf, BF16),
      *([ck] * p), *([cv] * p), *([clf] * p))


def _gla_kernel(gq_ref, gk_ref, gv_ref, gr_ref, gg_ref, s0_ref, gn_ref, tri_ref,
                o_ref, sfin_ref, st_ref, *, tm, chunk, seqs):
    i = pl.program_id(1)
    n_pairs = GLA_HEADS // 2
    pair = 2 * GLA_KEY_DIM

    @pl.when(i == 0)
    def _():
        for sq in range(seqs):
            for hp in range(n_pairs):
                st_ref[sq, hp] = s0_ref[sq, 2 * hp:2 * hp + 2].reshape(pair, GLA_VAL_DIM).T

    lane = lax.broadcasted_iota(jnp.int32, (chunk, pair), 1)
    rr = lax.broadcasted_iota(jnp.int32, (chunk, chunk), 0)
    cc = lax.broadcasted_iota(jnp.int32, (chunk, chunk), 1)
    causal = cc <= rr
    tri = tri_ref[...]
    gn = gn_ref[...]

    def chunk_body(c, carry):
        r0 = pl.multiple_of(c * chunk, chunk)
        rows = pl.ds(r0, chunk)
        for sq, hp in [(a, b) for a in range(seqs) for b in range(n_pairs)]:
            ks = slice(hp * pair, (hp + 1) * pair)
            g = gg_ref[sq, rows, ks]
            q = gq_ref[sq, rows, ks]
            k = gk_ref[sq, rows, ks]
            cb = jnp.dot(tri, jnp.concatenate(_split3(g), axis=1), preferred_element_type=F32)
            bc = cb[:, 0:pair] + cb[:, pair:2 * pair] + cb[:, 2 * pair:3 * pair]
            b_last = bc[chunk - 1:chunk, :]
            e_pos = jnp.exp(bc)
            qt = q * e_pos
            kt = (k * jnp.exp(-bc)).astype(BF16)
            kh = k * jnp.exp(b_last - bc)
            st = st_ref[sq, hp]
            st_bf = st.astype(BF16)
            st_new = st * jnp.exp(b_last)
            for hh in range(2):
                h = 2 * hp + hh
                own = (lane // GLA_KEY_DIM) == hh
                vs = slice(h * GLA_VAL_DIM, (h + 1) * GLA_VAL_DIM)
                v = gv_ref[sq, rows, vs].astype(BF16)
                qm = jnp.where(own, qt, 0.0).astype(BF16)
                a = lax.dot_general(qm, kt, (((1,), (1,)), ((), ())), preferred_element_type=F32)
                a = jnp.where(causal, a, 0.0).astype(BF16)
                o = jnp.dot(a, v, preferred_element_type=F32)
                o = o + lax.dot_general(qm, st_bf, (((1,), (1,)), ((), ())), preferred_element_type=F32)
                khm = jnp.where(own, kh, 0.0).astype(BF16)
                st_new = st_new + lax.dot_general(v, khm, (((0,), (0,)), ((), ())),
                                                  preferred_element_type=F32)
                on = o * lax.rsqrt(jnp.mean(o * o, axis=-1, keepdims=True) + EPS) * gn
                gate = gr_ref[sq, rows, vs]
                gate = gate * (1.0 / (1.0 + jnp.exp(-gate)))
                o_ref[sq, rows, vs] = (on * gate).astype(BF16)
            st_ref[sq, hp] = st_new
        return carry

    lax.fori_loop(0, tm // chunk, chunk_body, 0)

    @pl.when(i == pl.num_programs(1) - 1)
    def _():
        for sq in range(seqs):
            for hp in range(n_pairs):
                sfin_ref[sq, 2 * hp:2 * hp + 2] = st_ref[sq, hp].T.reshape(2, GLA_KEY_DIM, GLA_VAL_DIM)


def _gla(gq, gk, gv, gr, gg, s0, gla_out_norm_g, *, tm, chunk, seqs):
    nb, rows, _ = gq.shape
    nt = rows // tm
    assert nb % seqs == 0
    tri = np.tril(np.ones((chunk, chunk), np.float32))
    row_spec = lambda width: pl.BlockSpec((seqs, tm, width), lambda b, i: (b, i, 0))
    state_spec = pl.BlockSpec((seqs, GLA_HEADS, GLA_KEY_DIM, GLA_VAL_DIM), lambda b, i: (b, 0, 0, 0))
    return pl.pallas_call(
        functools.partial(_gla_kernel, tm=tm, chunk=chunk, seqs=seqs),
        grid=(nb // seqs, nt),
        in_specs=[row_spec(GLA_KDIM), row_spec(GLA_KDIM), row_spec(GLA_VDIM), row_spec(GLA_VDIM),
                  row_spec(GLA_KDIM), state_spec,
                  pl.BlockSpec((1, GLA_VAL_DIM), lambda b, i: (0, 0)),
                  pl.BlockSpec((chunk, chunk), lambda b, i: (0, 0))],
        out_specs=[row_spec(GLA_VDIM), state_spec],
        out_shape=[jax.ShapeDtypeStruct((nb, rows, GLA_VDIM), BF16),
                   jax.ShapeDtypeStruct((nb, GLA_HEADS, GLA_KEY_DIM, GLA_VAL_DIM), F32)],
        scratch_shapes=[pltpu.VMEM((seqs, GLA_HEADS // 2, GLA_VAL_DIM, 2 * GLA_KEY_DIM), F32)],
        compiler_params=pltpu.CompilerParams(dimension_semantics=("arbitrary", "arbitrary"),
                                             vmem_limit_bytes=VMEM_LIMIT),
        name="gla",
    )(gq, gk, gv, gr, gg, s0, gla_out_norm_g[None, :], jnp.asarray(tri, BF16))


def _out_mlp_kernel(x_ref, of_ref, og_ref, wof_ref, wog_ref, gm_ref, wup_ref, wdn_ref, y_ref, *, ff_chunk):
    x = x_ref[0]
    h = (x + jnp.dot(of_ref[0], wof_ref[...], preferred_element_type=F32)
         + jnp.dot(og_ref[0], wog_ref[...], preferred_element_type=F32))
    ms = jnp.mean(h * h, axis=-1, keepdims=True)
    hn = (h * lax.rsqrt(ms + EPS) * gm_ref[...]).astype(BF16)
    y_ref[0] = h
    for c in range(D_FF // ff_chunk):
        cs = slice(c * ff_chunk, (c + 1) * ff_chunk)
        u = jnp.maximum(jnp.dot(hn, wup_ref[:, cs], preferred_element_type=F32), 0.0)
        y_ref[0] += jnp.dot((u * u).astype(BF16), wdn_ref[cs, :], preferred_element_type=F32)


def _out_mlp(x, o_fox, o_gla, wof, wog, norm_mlp_g, w_up, w_down, *, fox_row_offset=0):
    nb, rows, _ = x.shape
    tm = ROW_TILE
    nt = rows // tm
    kf = o_fox.shape[-1]
    if fox_row_offset:
        assert fox_row_offset % 16 == 0, "row offset must keep bf16 sublane tiles aligned"
        shifted = lambda b, i: (b, pl.multiple_of(i * tm + fox_row_offset, 16), 0)
        of_spec = pl.BlockSpec((pl.Element(1), pl.Element(tm), pl.Element(kf)), shifted)
        og_spec = pl.BlockSpec((pl.Element(1), pl.Element(tm), pl.Element(GLA_VDIM)), shifted)
    else:
        of_spec = pl.BlockSpec((1, tm, kf), lambda b, i: (b, i, 0))
        og_spec = pl.BlockSpec((1, tm, GLA_VDIM), lambda b, i: (b, i, 0))
    return pl.pallas_call(
        functools.partial(_out_mlp_kernel, ff_chunk=D_MODEL),
        grid=(nb, nt),
        in_specs=[pl.BlockSpec((1, tm, D_MODEL), lambda b, i: (b, i, 0)), of_spec, og_spec,
                  _const_spec((kf, D_MODEL)), _const_spec((GLA_VDIM, D_MODEL)), _const_spec((1, D_MODEL)),
                  _const_spec((D_MODEL, D_FF)), _const_spec((D_FF, D_MODEL))],
        out_specs=pl.BlockSpec((1, tm, D_MODEL), lambda b, i: (b, i, 0)),
        out_shape=jax.ShapeDtypeStruct((nb, rows, D_MODEL), F32),
        compiler_params=pltpu.CompilerParams(dimension_semantics=("arbitrary", "arbitrary"),
                                             vmem_limit_bytes=VMEM_LIMIT),
        name="out_mlp",
    )(x, o_fox, o_gla, wof, wog, norm_mlp_g[None, :], w_up, w_down)


def kernel(x_prompt, x_sample, cache_k, cache_v, cache_logf, state_gla, page_table, meta_tokens, norm_mix_g,
           w_in, fox_b_f, fox_q_norm_g, fox_k_norm_g, gla_w_gup, gla_b_g, gla_out_norm_g, w_out, norm_mlp_g,
           w_up, w_down):
    depth = w_in.shape[0]
    assert depth == 1, "single-layer trunk"
    nb, seq, _ = x_prompt.shape
    n_seq, dec_seq, _ = x_sample.shape
    t_real = N_META + seq
    t_pad = -(-t_real // ROW_TILE) * ROW_TILE

    wts = _inproj_weights(norm_mix_g[0], w_in[0], fox_b_f[0], fox_q_norm_g[0], fox_k_norm_g[0],
                          gla_w_gup[0], gla_b_g[0])
    w_out_l = w_out[0]
    wof_aug = jnp.zeros((FOX_HEADS, LANES, D_MODEL), F32).at[:, :FOX_HEAD_DIM].set(
        w_out_l[:FOX_DIM].reshape(FOX_HEADS, FOX_HEAD_DIM, D_MODEL)).reshape(AUG_DIM, D_MODEL).astype(BF16)
    wof = w_out_l[:FOX_DIM].astype(BF16)
    wog = w_out_l[FOX_DIM:].astype(BF16)
    w_up_b = w_up[0].astype(BF16)
    w_dn_b = w_down[0].astype(BF16)

    meta = jnp.broadcast_to(meta_tokens[None].astype(x_prompt.dtype), (nb, N_META, D_MODEL))
    xp = jnp.concatenate([meta, x_prompt, jnp.zeros((nb, t_pad - t_real, D_MODEL), x_prompt.dtype)], axis=1)
    qt, ka, vt, k_p, v_p, lf_p, gq, gk, gv, gr, gg = _inproj(xp, wts, t_real=t_real, prompt=True)
    o_fox = _fox_prompt(qt, ka, vt)
    s0 = jnp.zeros((nb, GLA_HEADS, GLA_KEY_DIM, GLA_VAL_DIM), F32)
    o_gla, s_p = _gla(gq, gk, gv, gr, gg, s0, gla_out_norm_g[0], tm=ROW_TILE, chunk=GLA_CHUNK, seqs=nb)
    y_p = _out_mlp(x_prompt, o_fox, o_gla, wof_aug, wog, norm_mlp_g[0], w_up_b, w_dn_b,
                   fox_row_offset=N_META)

    xs = jnp.pad(x_sample, ((0, 0), (0, DECODE_ROWS - dec_seq), (0, 0))).reshape(1, n_seq * DECODE_ROWS, D_MODEL)
    qc, k_s, v_s, lf_s, hq, hk, hv, hr, hg = _inproj(xs, wts, t_real=dec_seq, prompt=False,
                                                      seq_rows=DECODE_ROWS)
    per_seq = lambda a: a.reshape(n_seq, DECODE_ROWS, a.shape[-1])
    head_of_lane = jnp.arange(FOX_DIM) // FOX_HEAD_DIM
    qmask = (head_of_lane[None, :] == jnp.arange(FOX_HEADS)[:, None]).astype(BF16)
    qbd = (per_seq(qc)[:, :, None, :] * qmask[None, None]).reshape(n_seq, DECODE_ROWS * FOX_HEADS, FOX_DIM)
    o_fox_s = _fox_decode(page_table, qbd, per_seq(k_s), per_seq(v_s), per_seq(lf_s),
                          cache_k[0], cache_v[0], cache_logf[0], t_new=dec_seq)
    o_gla_s, s_s = _gla(per_seq(hq), per_seq(hk), per_seq(hv), per_seq(hr), per_seq(hg),
                        state_gla[0].astype(F32), gla_out_norm_g[0], tm=DECODE_ROWS, chunk=DECODE_ROWS,
                        seqs=GLA_DECODE_SEQS_PER_STEP)
    y_s = _out_mlp(xs, o_fox_s.reshape(1, n_seq * DECODE_ROWS, FOX_DIM),
                   o_gla_s.reshape(1, n_seq * DECODE_ROWS, GLA_VDIM), wof, wog, norm_mlp_g[0], w_up_b, w_dn_b)

    real = lambda a: per_seq(a[0])[:, :dec_seq]
    return (y_p,
            real(y_s),
            jnp.transpose(k_p.reshape(1, nb, FOX_HEADS, FOX_HEAD_DIM, t_real), (0, 1, 4, 2, 3)),
            jnp.transpose(v_p.reshape(1, nb, FOX_HEADS, FOX_HEAD_DIM, t_real), (0, 1, 4, 2, 3)),
            jnp.transpose(lf_p, (0, 2, 1))[None],
            s_p[None],
            real(k_s).reshape(1, n_seq, dec_seq, FOX_HEADS, FOX_HEAD_DIM),
            real(v_s).reshape(1, n_seq, dec_seq, FOX_HEADS, FOX_HEAD_DIM),
            real(lf_s)[None],
            s_s[None])
```

```python
import functools

import jax
import jax.numpy as jnp
import numpy as np
from jax import lax
from jax.experimental import pallas as pl
from jax.experimental.pallas import tpu as pltpu

D_MODEL = 1024
N_META = 16
PAGE_SIZE = 128
FOX_HEADS = 8
FOX_HEAD_DIM = 64
FOX_DIM = FOX_HEADS * FOX_HEAD_DIM
GLA_HEADS = 4
GLA_KEY_DIM = 64
GLA_VAL_DIM = 128
GLA_KDIM = GLA_HEADS * GLA_KEY_DIM
GLA_VDIM = GLA_HEADS * GLA_VAL_DIM
GLA_GATE_RANK = 16
GLA_GATE_TEMP = 16.0
GLA_CHUNK = 64
D_FF = 4 * D_MODEL
EPS = 1e-6
LOG2E = 1.4426950408889634

LANES = 128
AUG_DIM = FOX_HEADS * LANES
AUG_ROWS = 16
V_ROWS = FOX_HEAD_DIM + AUG_ROWS
SMALL_COLS = LANES
ROW_TILE = 512
PROMPT_TILES_PER_TRIP = 8
DECODE_ROWS = 8
DECODE_PAGES_PER_STEP = 32
GLA_DECODE_SEQS_PER_STEP = 8
VMEM_LIMIT = 56 * 1024 * 1024

F32 = jnp.float32
BF16 = jnp.bfloat16

_C_FQ, _C_FK, _C_FV = 0, FOX_DIM, 2 * FOX_DIM
_C_GQ = 3 * FOX_DIM
_C_GK = _C_GQ + GLA_KDIM
_C_GV = _C_GK + GLA_KDIM
_C_GR = _C_GV + GLA_VDIM
_C_SM = _C_GR + GLA_VDIM
_W_COLS = _C_SM + SMALL_COLS


def _log_sigmoid(x):
    return jnp.minimum(x, 0.0) - jnp.log1p(jnp.exp(-jnp.abs(x)))


def _split3(x):
    p1 = x.astype(BF16)
    r = x - p1.astype(F32)
    p2 = r.astype(BF16)
    r = r - p2.astype(F32)
    return p1, p2, r.astype(BF16)


def _const_spec(shape):
    return pl.BlockSpec(shape, lambda *_: (0,) * len(shape), pipeline_mode=pl.Buffered(1))


def _inproj_kernel(x_ref, gmix_ref, w_ref, gmat_ref, qg_ref, kg_ref, bf_ref, wg_ref, bg_ref, *rest,
                   tm, t_real, seq_rows, prompt):
    i = pl.program_id(1)
    if prompt:
        (meta_ref, tri_ref, pk_ref, onek_ref, selq_ref, oneq_ref, onev_ref,
         qt_ref, ka_ref, vt_ref, ko_ref, vo_ref, lf_ref,
         gq_ref, gk_ref, gv_ref, gr_ref, gg_ref, carry_ref) = rest
        n_meta = meta_ref.shape[0]
        blk = x_ref[0]
        first = jnp.concatenate([meta_ref[...], blk[0:tm - n_meta]], axis=0)
        last = jnp.concatenate([blk[tm - n_meta:tm], jnp.zeros((tm - n_meta, D_MODEL), F32)], axis=0)
        x = jnp.where(i == 0, first, jnp.where(i == pl.num_programs(1) - 1, last, blk))
    else:
        (qc_ref, ko_ref, vo_ref, lf_ref, gq_ref, gk_ref, gv_ref, gr_ref, gg_ref) = rest
        x = x_ref[0]

    ms = jnp.mean(x * x, axis=-1, keepdims=True)
    xn = (x * lax.rsqrt(ms + EPS) * gmix_ref[...]).astype(BF16)
    z = jnp.dot(xn, w_ref[...], preferred_element_type=F32)

    def head_norm(zz, g_row):
        msq = jnp.dot((zz * zz).astype(BF16), gmat_ref[...], preferred_element_type=F32)
        return zz * lax.rsqrt(msq + EPS) * g_row

    qn = head_norm(z[:, _C_FQ:_C_FQ + FOX_DIM], qg_ref[...])
    kn = head_norm(z[:, _C_FK:_C_FK + FOX_DIM], kg_ref[...])
    zv = z[:, _C_FV:_C_FV + FOX_DIM]
    zs = z[:, _C_SM:_C_SM + SMALL_COLS]
    lf = _log_sigmoid(zs + bf_ref[...])
    if prompt:
        ko_ref[0] = kn.T
        zv_t = zv.T
        vo_ref[0] = zv_t
        lf_ref[0] = lf.T[0:FOX_HEADS, :]
    else:
        ko_ref[0] = kn
        vo_ref[0] = zv
        lf_ref[0] = lf[:, 0:FOX_HEADS]

    row = i * tm + lax.broadcasted_iota(jnp.int32, (tm, 1), 0)
    if prompt:
        valid = row < t_real
    else:
        valid = (row % seq_rows) < t_real

    gq_ref[0] = z[:, _C_GQ:_C_GQ + GLA_KDIM] * (GLA_KEY_DIM ** -0.5)
    gk_ref[0] = z[:, _C_GK:_C_GK + GLA_KDIM]
    gv_ref[0] = z[:, _C_GV:_C_GV + GLA_VDIM]
    gr_ref[0] = z[:, _C_GR:_C_GR + GLA_VDIM]
    xg = jnp.dot(zs.astype(BF16), wg_ref[...], preferred_element_type=F32) + bg_ref[...]
    gg_ref[0] = jnp.where(valid, _log_sigmoid(xg) * (1.0 / GLA_GATE_TEMP), 0.0)

    if not prompt:
        qc_ref[0] = qn.astype(BF16)
        return

    @pl.when(i == 0)
    def _():
        carry_ref[...] = jnp.zeros_like(carry_ref)

    lane = lax.broadcasted_iota(jnp.int32, (tm, SMALL_COLS), 1)
    lfm = jnp.where(lane < FOX_HEADS, lf, 0.0)
    c = jnp.dot(tri_ref[...], jnp.concatenate(_split3(lfm), axis=1), preferred_element_type=F32)
    fcum = c[:, 0:LANES] + c[:, LANES:2 * LANES] + c[:, 2 * LANES:3 * LANES] + carry_ref[...]
    carry_ref[...] = fcum[tm - 1:tm, :]
    fs = fcum * LOG2E
    a1, a2, a3 = _split3(fs)

    k_cat = jnp.concatenate([kn.astype(BF16), -a1, -a2, -a3], axis=1)
    ka = jnp.dot(k_cat, pk_ref[...], preferred_element_type=F32) + onek_ref[...]
    ka_ref[0] = ka.astype(BF16)

    qn_t = qn.T
    aug_q = jnp.dot(selq_ref[...], jnp.concatenate(_split3(fs.T), axis=0),
                    preferred_element_type=F32) + oneq_ref[...]
    zero_rows = jnp.zeros((LANES - FOX_HEAD_DIM - AUG_ROWS, tm), BF16)
    for h in range(FOX_HEADS):
        hs = slice(h * FOX_HEAD_DIM, (h + 1) * FOX_HEAD_DIM)
        qt_ref[0, h, 0:FOX_HEAD_DIM, :] = qn_t[hs].astype(BF16)
        qt_ref[0, h, FOX_HEAD_DIM:FOX_HEAD_DIM + AUG_ROWS, :] = aug_q[h * AUG_ROWS:(h + 1) * AUG_ROWS].astype(BF16)
        qt_ref[0, h, FOX_HEAD_DIM + AUG_ROWS:LANES, :] = zero_rows
        vt_ref[0, h, 0:FOX_HEAD_DIM, :] = zv_t[hs].astype(BF16)
        vt_ref[0, h, FOX_HEAD_DIM:FOX_HEAD_DIM + AUG_ROWS, :] = jnp.broadcast_to(
            onev_ref[...], (AUG_ROWS, tm)).astype(BF16)


def _placement_constants():
    n_cat = FOX_DIM + 3 * LANES
    pk = np.zeros((n_cat, AUG_DIM), np.float32)
    onek = np.zeros((1, AUG_DIM), np.float32)
    selq = np.zeros((FOX_HEADS * AUG_ROWS, 3 * LANES), np.float32)
    oneq = np.zeros((FOX_HEADS * AUG_ROWS, 1), np.float32)
    onev = np.zeros((AUG_ROWS, 1), np.float32)
    onev[0, 0] = 1.0
    for h in range(FOX_HEADS):
        base = h * LANES
        for d in range(FOX_HEAD_DIM):
            pk[h * FOX_HEAD_DIM + d, base + d] = 1.0
        for piece in range(3):
            pk[FOX_DIM + piece * LANES + h, base + FOX_HEAD_DIM + piece] = 1.0
            onek[0, base + FOX_HEAD_DIM + 3 + piece] = 1.0
            oneq[h * AUG_ROWS + piece, 0] = 1.0
            selq[h * AUG_ROWS + 3 + piece, piece * LANES + h] = 1.0
    gmat = np.kron(np.eye(FOX_HEADS, dtype=np.float32),
                   np.full((FOX_HEAD_DIM, FOX_HEAD_DIM), 1.0 / FOX_HEAD_DIM, np.float32))
    return pk, onek, selq, oneq, onev, gmat


def _inproj_weights(norm_mix_g, w_in, fox_b_f, fox_q_norm_g, fox_k_norm_g, gla_w_gup, gla_b_g):
    o = 0
    cols = {}
    for name, width in (("fq", FOX_DIM), ("fk", FOX_DIM), ("fv", FOX_DIM), ("ff", FOX_HEADS),
                        ("gq", GLA_KDIM), ("gk", GLA_KDIM), ("gv", GLA_VDIM), ("gr", GLA_VDIM),
                        ("glr", GLA_GATE_RANK)):
        cols[name] = w_in[:, o:o + width]
        o += width
    pad = jnp.zeros((D_MODEL, SMALL_COLS - FOX_HEADS - GLA_GATE_RANK), w_in.dtype)
    w = jnp.concatenate([cols["fq"], cols["fk"], cols["fv"], cols["gq"], cols["gk"], cols["gv"],
                         cols["gr"], cols["ff"], cols["glr"], pad], axis=1).astype(BF16)
    wg = jnp.zeros((SMALL_COLS, GLA_KDIM), F32).at[FOX_HEADS:FOX_HEADS + GLA_GATE_RANK].set(gla_w_gup)
    bf = jnp.zeros((1, SMALL_COLS), F32).at[0, :FOX_HEADS].set(fox_b_f)
    qg = jnp.tile(fox_q_norm_g, FOX_HEADS)[None, :] * (FOX_HEAD_DIM ** -0.5 * LOG2E)
    kg = jnp.tile(fox_k_norm_g, FOX_HEADS)[None, :]
    return dict(gmix=norm_mix_g[None, :], w=w, qg=qg, kg=kg, bf=bf, wg=wg.astype(BF16),
                bg=gla_b_g[None, :])


def _inproj(x, wts, *, t_real, prompt, seq_rows=0, meta=None):
    nb, x_rows, _ = x.shape
    tm = ROW_TILE
    row_spec = lambda width: pl.BlockSpec((1, tm, width), lambda b, i: (b, i, 0))
    if prompt:
        n_meta = meta.shape[0]
        assert x_rows % tm == 0 and t_real == n_meta + x_rows and n_meta % 16 == 0 and n_meta < tm
        rows = x_rows + tm
        x_spec = pl.BlockSpec(
            (pl.Element(1), pl.Element(tm), pl.Element(D_MODEL)),
            lambda b, i: (b, pl.multiple_of(jnp.clip(i * tm - n_meta, 0, x_rows - tm), 16), 0))
    else:
        rows = x_rows
        x_spec = row_spec(D_MODEL)
    nt = rows // tm
    pk, onek, selq, oneq, onev, gmat = _placement_constants()
    in_specs = [x_spec, _const_spec((1, D_MODEL)), _const_spec((D_MODEL, _W_COLS)),
                _const_spec((FOX_DIM, FOX_DIM)), _const_spec((1, FOX_DIM)), _const_spec((1, FOX_DIM)),
                _const_spec((1, SMALL_COLS)), _const_spec((SMALL_COLS, GLA_KDIM)),
                _const_spec((1, GLA_KDIM))]
    args = [x, wts["gmix"], wts["w"], jnp.asarray(gmat, BF16), wts["qg"], wts["kg"], wts["bf"],
            wts["wg"], wts["bg"]]
    gla_shapes = [jax.ShapeDtypeStruct((nb, rows, GLA_KDIM), F32),
                  jax.ShapeDtypeStruct((nb, rows, GLA_KDIM), F32),
                  jax.ShapeDtypeStruct((nb, rows, GLA_VDIM), F32),
                  jax.ShapeDtypeStruct((nb, rows, GLA_VDIM), F32),
                  jax.ShapeDtypeStruct((nb, rows, GLA_KDIM), F32)]
    gla_specs = [row_spec(GLA_KDIM), row_spec(GLA_KDIM), row_spec(GLA_VDIM), row_spec(GLA_VDIM),
                 row_spec(GLA_KDIM)]
    scratch = []
    if prompt:
        n_cat = FOX_DIM + 3 * LANES
        tri = np.tril(np.ones((tm, tm), np.float32))
        in_specs += [_const_spec((n_meta, D_MODEL)),
                     _const_spec((tm, tm)), _const_spec((n_cat, AUG_DIM)), _const_spec((1, AUG_DIM)),
                     _const_spec((FOX_HEADS * AUG_ROWS, 3 * LANES)), _const_spec((FOX_HEADS * AUG_ROWS, 1)),
                     _const_spec((AUG_ROWS, 1))]
        args += [meta, jnp.asarray(tri, BF16), jnp.asarray(pk, BF16), jnp.asarray(onek),
                 jnp.asarray(selq, BF16), jnp.asarray(oneq), jnp.asarray(onev)]
        head_spec = lambda height: pl.BlockSpec((1, FOX_HEADS, height, tm), lambda b, i: (b, 0, 0, i))
        out_shape = [jax.ShapeDtypeStruct((nb, FOX_HEADS, LANES, rows), BF16),
                     jax.ShapeDtypeStruct((nb, rows, AUG_DIM), BF16),
                     jax.ShapeDtypeStruct((nb, FOX_HEADS, V_ROWS, rows), BF16)]
        out_specs = [head_spec(LANES), row_spec(AUG_DIM), head_spec(V_ROWS)]
        col_spec = lambda height: pl.BlockSpec((1, height, tm), lambda b, i: (b, 0, i))
        out_shape += [jax.ShapeDtypeStruct((nb, FOX_DIM, t_real), F32),
                      jax.ShapeDtypeStruct((nb, FOX_DIM, t_real), F32),
                      jax.ShapeDtypeStruct((nb, FOX_HEADS, t_real), F32)]
        out_specs += [col_spec(FOX_DIM), col_spec(FOX_DIM), col_spec(FOX_HEADS)]
        scratch = [pltpu.VMEM((1, SMALL_COLS), F32)]
    else:
        out_shape = [jax.ShapeDtypeStruct((nb, rows, FOX_DIM), BF16),
                     jax.ShapeDtypeStruct((nb, rows, FOX_DIM), F32),
                     jax.ShapeDtypeStruct((nb, rows, FOX_DIM), F32),
                     jax.ShapeDtypeStruct((nb, rows, FOX_HEADS), F32)]
        out_specs = [row_spec(FOX_DIM), row_spec(FOX_DIM), row_spec(FOX_DIM), row_spec(FOX_HEADS)]
    out_shape += gla_shapes
    out_specs += gla_specs
    return pl.pallas_call(
        functools.partial(_inproj_kernel, tm=tm, t_real=t_real, seq_rows=seq_rows, prompt=prompt),
        grid=(nb, nt),
        in_specs=in_specs,
        out_specs=out_specs,
        out_shape=out_shape,
        scratch_shapes=scratch,
        compiler_params=pltpu.CompilerParams(dimension_semantics=("arbitrary", "arbitrary"),
                                             vmem_limit_bytes=VMEM_LIMIT),
        name="inproj_prompt" if prompt else "inproj_decode",
    )(*args)


def _fox_prompt_kernel(qt_ref, k_ref, vt_ref, o_ref, sa_ref, sb_ref, ma_ref, mb_ref, m_ref, acc_ref, *, tile):
    i = pl.program_id(2)
    qt = qt_ref[0, 0]
    m_ref[...] = jnp.full_like(m_ref, -jnp.inf)
    acc_ref[...] = jnp.zeros_like(acc_ref)
    bufs = ((sa_ref, ma_ref), (sb_ref, mb_ref))

    def scores(t, parity, causal):
        s_ref, mt_ref = bufs[parity]
        k = k_ref[0, pl.ds(pl.multiple_of(t * tile, tile), tile), :]
        s = jnp.dot(k, qt, preferred_element_type=F32)
        if causal:
            r = lax.broadcasted_iota(jnp.int32, (tile, tile), 0)
            c = lax.broadcasted_iota(jnp.int32, (tile, tile), 1)
            s = jnp.where(r <= c, s, -jnp.inf)
        s_ref[...] = s
        mt_ref[...] = jnp.max(s, axis=0, keepdims=True)

    def accumulate(t, parity):
        s_ref, mt_ref = bufs[parity]
        vt = vt_ref[0, 0, :, pl.ds(pl.multiple_of(t * tile, tile), tile)]
        m_old = m_ref[...]
        m_new = jnp.maximum(m_old, mt_ref[...])
        p = jnp.exp2(s_ref[...] - m_new)
        acc_ref[...] = acc_ref[...] * jnp.exp2(m_old - m_new) + jnp.dot(
            vt, p.astype(BF16), preferred_element_type=F32)
        m_ref[...] = m_new

    def step(t, parity, causal_next):
        scores(t + 1, 1 - parity, causal_next)
        accumulate(t, parity)

    odd = (i % 2) == 1

    @pl.when(i == 0)
    def _():
        scores(0, 0, True)
        accumulate(0, 0)

    @pl.when(i > 0)
    def _():
        scores(0, 0, False)
        unroll = PROMPT_TILES_PER_TRIP
        n_trips = (i - 1) // unroll

        def trip(jj, carry):
            for u in range(unroll):
                step(unroll * jj + u, u % 2, False)
            return carry

        lax.fori_loop(0, n_trips, trip, 0)
        t0 = unroll * n_trips
        pre = i - t0 - jnp.where(odd, 1, 2)

        @pl.when(pre >= 4)
        def _():
            for u in range(4):
                step(t0 + u, u % 2, False)

        t1 = t0 + jnp.where(pre >= 4, 4, 0)

        @pl.when(pre % 4 == 2)
        def _():
            step(t1, 0, False)
            step(t1 + 1, 1, False)

        @pl.when(odd)
        def _():
            step(i - 1, 0, True)
            accumulate(i, 1)

        @pl.when(jnp.logical_not(odd))
        def _():
            step(i - 2, 0, False)
            step(i - 1, 1, True)
            accumulate(i, 0)

    acc = acc_ref[...]
    o_t = acc[0:FOX_HEAD_DIM] / acc[FOX_HEAD_DIM:FOX_HEAD_DIM + 1]
    o_t = jnp.concatenate([o_t, jnp.zeros((LANES - FOX_HEAD_DIM, tile), F32)], axis=0)
    o_ref[0] = o_t.T.astype(BF16)


def _fox_prompt(qt, ka, vt):
    nb, rows, _ = ka.shape
    tile = ROW_TILE
    nq = rows // tile
    return pl.pallas_call(
        functools.partial(_fox_prompt_kernel, tile=tile),
        grid=(nb, FOX_HEADS, nq),
        in_specs=[pl.BlockSpec((1, 1, LANES, tile), lambda b, h, i: (b, h, 0, i)),
                  pl.BlockSpec((1, rows, LANES), lambda b, h, i: (b, 0, h)),
                  pl.BlockSpec((1, 1, V_ROWS, rows), lambda b, h, i: (b, h, 0, 0))],
        out_specs=pl.BlockSpec((1, tile, LANES), lambda b, h, i: (b, i, h)),
        out_shape=jax.ShapeDtypeStruct((nb, rows, AUG_DIM), BF16),
        scratch_shapes=[pltpu.VMEM((tile, tile), F32), pltpu.VMEM((tile, tile), F32),
                        pltpu.VMEM((1, tile), F32), pltpu.VMEM((1, tile), F32),
                        pltpu.VMEM((1, tile), F32), pltpu.VMEM((V_ROWS, tile), F32)],
        compiler_params=pltpu.CompilerParams(
            dimension_semantics=("arbitrary", "arbitrary", "arbitrary"), vmem_limit_bytes=VMEM_LIMIT),
        name="fox_prompt",
    )(qt, ka, vt)


def _fox_decode_kernel(pt_ref, qbd_ref, kn_ref, vn_ref, lfn_ref, usuf_ref, *rest, n_pages, t_new):
    del pt_ref
    p = n_pages
    k_refs, v_refs, lf_refs = rest[0:p], rest[p:2 * p], rest[2 * p:3 * p]
    o_ref, m_ref, l_ref, acc_ref, carry_ref, qoff_ref = rest[3 * p:]
    j = pl.program_id(1)
    nq = DECODE_ROWS * FOX_HEADS
    qbd = qbd_ref[0]

    def suffix_sums(lf, token_minor):
        n = lf.shape[1] if token_minor else lf.shape[0]
        u = usuf_ref[0:n, :]
        u = jnp.concatenate([u[:, 0:n], u[:, PAGE_SIZE:PAGE_SIZE + LANES]], axis=1)
        if token_minor:
            r = jnp.dot(jnp.concatenate(_split3(lf), axis=0), u, preferred_element_type=F32)
        else:
            r = lax.dot_general(jnp.concatenate(_split3(lf), axis=1), u, (((0,), (0,)), ((), ())),
                                preferred_element_type=F32)
        r = r[0:FOX_HEADS] + r[FOX_HEADS:2 * FOX_HEADS] + r[2 * FOX_HEADS:3 * FOX_HEADS]
        return r[:, 0:n], r[:, n:n + LANES]

    def attend(k, v, bias_t, causal, token_minor):
        k = k.astype(BF16)
        v = v.astype(BF16)
        if token_minor:
            n = k.shape[1]
            s = jnp.dot(qbd, k, preferred_element_type=F32)
        else:
            n = k.shape[0]
            s = lax.dot_general(qbd, k, (((1,), (1,)), ((), ())), preferred_element_type=F32)
        s = s + jnp.concatenate([bias_t] * DECODE_ROWS, axis=0) - qoff_ref[...]
        if causal:
            r = lax.broadcasted_iota(jnp.int32, (nq, n), 0)
            c = lax.broadcasted_iota(jnp.int32, (nq, n), 1)
            s = jnp.where(c * FOX_HEADS <= r, s, -jnp.inf)
        m_old = m_ref[...]
        m_new = jnp.maximum(m_old, jnp.max(s, axis=-1, keepdims=True))
        alpha = jnp.exp2(m_old - m_new)
        pr = jnp.exp2(s - m_new)
        l_ref[...] = l_ref[...] * alpha + jnp.sum(pr, axis=-1, keepdims=True)
        pv_dims = (((1,), (1,)), ((), ())) if token_minor else (((1,), (0,)), ((), ()))
        acc_ref[...] = acc_ref[...] * alpha + lax.dot_general(pr.astype(BF16), v, pv_dims,
                                                              preferred_element_type=F32)
        m_ref[...] = m_new

    @pl.when(j == 0)
    def _():
        m_ref[...] = jnp.full_like(m_ref, -jnp.inf)
        l_ref[...] = jnp.zeros_like(l_ref)
        acc_ref[...] = jnp.zeros_like(acc_ref)
        rowi = lax.broadcasted_iota(jnp.int32, (DECODE_ROWS, FOX_HEADS), 0)
        lf = jnp.where(rowi < t_new, lfn_ref[0], 0.0) * LOG2E
        r_t, tot = suffix_sums(lf, False)
        rt_rows = jnp.concatenate([r_t] * DECODE_ROWS, axis=0)
        r = lax.broadcasted_iota(jnp.int32, (nq, DECODE_ROWS), 0)
        c = lax.broadcasted_iota(jnp.int32, (nq, DECODE_ROWS), 1)
        own_t = jnp.logical_and(c * FOX_HEADS <= r, r < (c + 1) * FOX_HEADS)
        qoff_ref[...] = jnp.sum(jnp.where(own_t, rt_rows, 0.0), axis=-1, keepdims=True)
        attend(kn_ref[0], vn_ref[0], r_t, True, False)
        carry_ref[...] = tot

    n_piece_rows = 3 * FOX_HEADS
    lf_stack = jnp.concatenate(
        [piece for idx in range(p) for piece in _split3(lf_refs[idx][...] * LOG2E)], axis=0)
    r_all = jnp.dot(lf_stack, usuf_ref[...], preferred_element_type=F32)
    carry = carry_ref[...]
    biases = []
    for idx in range(p):
        r = r_all[idx * n_piece_rows:(idx + 1) * n_piece_rows]
        r = r[0:FOX_HEADS] + r[FOX_HEADS:2 * FOX_HEADS] + r[2 * FOX_HEADS:3 * FOX_HEADS]
        biases.append(r[:, 0:PAGE_SIZE] + carry)
        carry = carry + r[:, PAGE_SIZE:PAGE_SIZE + LANES]
    carry_ref[...] = carry
    attend(jnp.concatenate([k_refs[idx][...].astype(BF16) for idx in range(p)], axis=1),
           jnp.concatenate([v_refs[idx][...].astype(BF16) for idx in range(p)], axis=1),
           jnp.concatenate(biases, axis=1), False, True)

    @pl.when(j == pl.num_programs(1) - 1)
    def _():
        o = acc_ref[...] / l_ref[...]
        r = lax.broadcasted_iota(jnp.int32, (nq, FOX_DIM), 0)
        c = lax.broadcasted_iota(jnp.int32, (nq, FOX_DIM), 1)
        own = (c // FOX_HEAD_DIM) == (r % FOX_HEADS)
        o = jnp.where(own, o, 0.0).reshape(DECODE_ROWS, FOX_HEADS, FOX_DIM)
        o_ref[0] = jnp.sum(o, axis=1).astype(BF16)


def _fox_decode(page_table, qbd, k_new, v_new, lf_new, cache_k, cache_v, cache_logf, *, t_new):
    n_seq, n_pages_seq = page_table.shape
    p = DECODE_PAGES_PER_STEP
    n_steps = n_pages_seq // p
    nq = DECODE_ROWS * FOX_HEADS
    n_pool = cache_k.shape[0]
    ck = jnp.transpose(cache_k, (0, 2, 3, 1)).reshape(n_pool, FOX_DIM, PAGE_SIZE)
    cv = jnp.transpose(cache_v, (0, 2, 3, 1)).reshape(n_pool, FOX_DIM, PAGE_SIZE)
    clf = jnp.transpose(cache_logf, (0, 2, 1))
    usuf = np.concatenate([np.tril(np.ones((PAGE_SIZE, PAGE_SIZE), np.float32), -1),
                           np.ones((PAGE_SIZE, LANES), np.float32)], axis=1)

    def page_map(idx):
        return lambda n, j, pt: (pt[n * n_pages_seq + n_pages_seq - 1 - (j * p + idx)], 0, 0)

    seq_spec = lambda r, w: pl.BlockSpec((1, r, w), lambda n, j, pt: (n, 0, 0))
    in_specs = [seq_spec(nq, FOX_DIM), seq_spec(DECODE_ROWS, FOX_DIM), seq_spec(DECODE_ROWS, FOX_DIM),
                seq_spec(DECODE_ROWS, FOX_HEADS),
                pl.BlockSpec((PAGE_SIZE, PAGE_SIZE + LANES), lambda n, j, pt: (0, 0))]
    in_specs += [pl.BlockSpec((None, FOX_DIM, PAGE_SIZE), page_map(idx)) for idx in range(p)]
    in_specs += [pl.BlockSpec((None, FOX_DIM, PAGE_SIZE), page_map(idx)) for idx in range(p)]
    in_specs += [pl.BlockSpec((None, FOX_HEADS, PAGE_SIZE), page_map(idx)) for idx in range(p)]
    grid_spec = pltpu.PrefetchScalarGridSpec(
        num_scalar_prefetch=1,
        grid=(n_seq, n_steps),
        in_specs=in_specs,
        out_specs=pl.BlockSpec((1, DECODE_ROWS, FOX_DIM), lambda n, j, pt: (n, 0, 0)),
        scratch_shapes=[pltpu.VMEM((nq, 1), F32), pltpu.VMEM((nq, 1), F32), pltpu.VMEM((nq, FOX_DIM), F32),
                        pltpu.VMEM((FOX_HEADS, LANES), F32), pltpu.VMEM((nq, 1), F32)],
    )
    return pl.pallas_call(
        functools.partial(_fox_decode_kernel, n_pages=p, t_new=t_new),
        grid_spec=grid_spec,
        out_shape=jax.ShapeDtypeStruct((n_seq, DECODE_ROWS, FOX_DIM), BF16),
        compiler_params=pltpu.CompilerParams(dimension_semantics=("arbitrary", "arbitrary"),
                                             vmem_limit_bytes=VMEM_LIMIT),
        name="fox_decode",
    )(page_table.reshape(-1), qbd, k_new, v_new, lf_new, jnp.asarray(usuf, BF16),
      *([ck] * p), *([cv] * p), *([clf] * p))


def _gla_kernel(gq_ref, gk_ref, gv_ref, gr_ref, gg_ref, s0_ref, gn_ref, tri_ref,
                o_ref, sfin_ref, st_ref, *, tm, chunk, seqs):
    i = pl.program_id(1)
    n_pairs = GLA_HEADS // 2
    pair = 2 * GLA_KEY_DIM

    @pl.when(i == 0)
    def _():
        for sq in range(seqs):
            for hp in range(n_pairs):
                st_ref[sq, hp] = s0_ref[sq, 2 * hp:2 * hp + 2].reshape(pair, GLA_VAL_DIM).T

    nch = tm // chunk
    lane = lax.broadcasted_iota(jnp.int32, (tm, pair), 1)
    rr = lax.broadcasted_iota(jnp.int32, (tm, tm), 0)
    cc = lax.broadcasted_iota(jnp.int32, (tm, tm), 1)
    in_chunk = jnp.logical_and(cc <= rr, cc >= rr - rr % chunk)
    tri = tri_ref[...]
    gn = gn_ref[...]
    by_chunk = lambda a: a.reshape(nch, chunk, a.shape[-1])

    for sq, hp in [(a, b) for a in range(seqs) for b in range(n_pairs)]:
        ks = slice(hp * pair, (hp + 1) * pair)
        g = gg_ref[sq, :, ks]
        q = gq_ref[sq, :, ks]
        k = gk_ref[sq, :, ks]
        cb = jnp.dot(tri, jnp.concatenate(_split3(g), axis=1), preferred_element_type=F32)
        bc = cb[:, 0:pair] + cb[:, pair:2 * pair] + cb[:, 2 * pair:3 * pair]
        bc3 = by_chunk(bc)
        b_last = bc3[:, chunk - 1:chunk, :]
        qt = q * jnp.exp(bc)
        kt = (k * jnp.exp(-bc)).astype(BF16)
        kh = (by_chunk(k) * jnp.exp(b_last - bc3)).reshape(tm, pair)
        decay = jnp.exp(b_last)
        heads = []
        upd = None
        for hh in range(2):
            own = (lane // GLA_KEY_DIM) == hh
            vs = slice((2 * hp + hh) * GLA_VAL_DIM, (2 * hp + hh + 1) * GLA_VAL_DIM)
            v = gv_ref[sq, :, vs].astype(BF16)
            qm = jnp.where(own, qt, 0.0).astype(BF16)
            khm = jnp.where(own, kh, 0.0).astype(BF16)
            u = lax.dot_general(by_chunk(v), by_chunk(khm), (((1,), (1,)), ((0,), (0,))),
                                preferred_element_type=F32)
            upd = u if upd is None else upd + u
            heads.append((vs, v, qm))
        st = st_ref[sq, hp]
        entering = []
        for c in range(nch):
            entering.append(st)
            st = st * decay[c] + upd[c]
        st_ref[sq, hp] = st
        st_in = jnp.stack(entering).astype(BF16)
        for vs, v, qm in heads:
            a = lax.dot_general(qm, kt, (((1,), (1,)), ((), ())), preferred_element_type=F32)
            a = jnp.where(in_chunk, a, 0.0).astype(BF16)
            o = jnp.dot(a, v, preferred_element_type=F32)
            o = o + lax.dot_general(by_chunk(qm), st_in, (((2,), (2,)), ((0,), (0,))),
                                    preferred_element_type=F32).reshape(tm, GLA_VAL_DIM)
            on = o * lax.rsqrt(jnp.mean(o * o, axis=-1, keepdims=True) + EPS) * gn
            gate = gr_ref[sq, :, vs]
            gate = gate * (1.0 / (1.0 + jnp.exp(-gate)))
            o_ref[sq, :, vs] = (on * gate).astype(BF16)

    @pl.when(i == pl.num_programs(1) - 1)
    def _():
        for sq in range(seqs):
            for hp in range(n_pairs):
                sfin_ref[sq, 2 * hp:2 * hp + 2] = st_ref[sq, hp].T.reshape(2, GLA_KEY_DIM, GLA_VAL_DIM)


def _gla(gq, gk, gv, gr, gg, s0, gla_out_norm_g, *, tm, chunk, seqs):
    nb, rows, _ = gq.shape
    nt = rows // tm
    assert nb % seqs == 0
    tri = np.kron(np.eye(tm // chunk, dtype=np.float32), np.tril(np.ones((chunk, chunk), np.float32)))
    row_spec = lambda width: pl.BlockSpec((seqs, tm, width), lambda b, i: (b, i, 0))
    state_spec = pl.BlockSpec((seqs, GLA_HEADS, GLA_KEY_DIM, GLA_VAL_DIM), lambda b, i: (b, 0, 0, 0))
    return pl.pallas_call(
        functools.partial(_gla_kernel, tm=tm, chunk=chunk, seqs=seqs),
        grid=(nb // seqs, nt),
        in_specs=[row_spec(GLA_KDIM), row_spec(GLA_KDIM), row_spec(GLA_VDIM), row_spec(GLA_VDIM),
                  row_spec(GLA_KDIM), state_spec,
                  pl.BlockSpec((1, GLA_VAL_DIM), lambda b, i: (0, 0)),
                  pl.BlockSpec((tm, tm), lambda b, i: (0, 0))],
        out_specs=[row_spec(GLA_VDIM), state_spec],
        out_shape=[jax.ShapeDtypeStruct((nb, rows, GLA_VDIM), BF16),
                   jax.ShapeDtypeStruct((nb, GLA_HEADS, GLA_KEY_DIM, GLA_VAL_DIM), F32)],
        scratch_shapes=[pltpu.VMEM((seqs, GLA_HEADS // 2, GLA_VAL_DIM, 2 * GLA_KEY_DIM), F32)],
        compiler_params=pltpu.CompilerParams(dimension_semantics=("arbitrary", "arbitrary"),
                                             vmem_limit_bytes=VMEM_LIMIT),
        name="gla",
    )(gq, gk, gv, gr, gg, s0, gla_out_norm_g[None, :], jnp.asarray(tri, BF16))


def _out_mlp_kernel(x_ref, of_ref, og_ref, wof_ref, wog_ref, gm_ref, wup_ref, wdn_ref, y_ref, *, ff_chunk):
    x = x_ref[0]
    h = (x + jnp.dot(of_ref[0], wof_ref[...], preferred_element_type=F32)
         + jnp.dot(og_ref[0], wog_ref[...], preferred_element_type=F32))
    ms = jnp.mean(h * h, axis=-1, keepdims=True)
    hn = (h * lax.rsqrt(ms + EPS) * gm_ref[...]).astype(BF16)
    y_ref[0] = h
    for c in range(D_FF // ff_chunk):
        cs = slice(c * ff_chunk, (c + 1) * ff_chunk)
        u = jnp.maximum(jnp.dot(hn, wup_ref[:, cs], preferred_element_type=F32), 0.0)
        y_ref[0] += jnp.dot((u * u).astype(BF16), wdn_ref[cs, :], preferred_element_type=F32)


def _out_mlp(x, o_fox, o_gla, wof, wog, norm_mlp_g, w_up, w_down, *, fox_row_offset=0):
    nb, rows, _ = x.shape
    tm = ROW_TILE
    nt = rows // tm
    kf = o_fox.shape[-1]
    if fox_row_offset:
        assert fox_row_offset % 16 == 0, "row offset must keep bf16 sublane tiles aligned"
        shifted = lambda b, i: (b, pl.multiple_of(i * tm + fox_row_offset, 16), 0)
        of_spec = pl.BlockSpec((pl.Element(1), pl.Element(tm), pl.Element(kf)), shifted)
        og_spec = pl.BlockSpec((pl.Element(1), pl.Element(tm), pl.Element(GLA_VDIM)), shifted)
    else:
        of_spec = pl.BlockSpec((1, tm, kf), lambda b, i: (b, i, 0))
        og_spec = pl.BlockSpec((1, tm, GLA_VDIM), lambda b, i: (b, i, 0))
    return pl.pallas_call(
        functools.partial(_out_mlp_kernel, ff_chunk=D_MODEL),
        grid=(nb, nt),
        in_specs=[pl.BlockSpec((1, tm, D_MODEL), lambda b, i: (b, i, 0)), of_spec, og_spec,
                  _const_spec((kf, D_MODEL)), _const_spec((GLA_VDIM, D_MODEL)), _const_spec((1, D_MODEL)),
                  _const_spec((D_MODEL, D_FF)), _const_spec((D_FF, D_MODEL))],
        out_specs=pl.BlockSpec((1, tm, D_MODEL), lambda b, i: (b, i, 0)),
        out_shape=jax.ShapeDtypeStruct((nb, rows, D_MODEL), F32),
        compiler_params=pltpu.CompilerParams(dimension_semantics=("arbitrary", "arbitrary"),
                                             vmem_limit_bytes=VMEM_LIMIT),
        name="out_mlp",
    )(x, o_fox, o_gla, wof, wog, norm_mlp_g[None, :], w_up, w_down)


def kernel(x_prompt, x_sample, cache_k, cache_v, cache_logf, state_gla, page_table, meta_tokens, norm_mix_g,
           w_in, fox_b_f, fox_q_norm_g, fox_k_norm_g, gla_w_gup, gla_b_g, gla_out_norm_g, w_out, norm_mlp_g,
           w_up, w_down):
    depth = w_in.shape[0]
    assert depth == 1, "single-layer trunk"
    nb, seq, _ = x_prompt.shape
    n_seq, dec_seq, _ = x_sample.shape
    t_real = N_META + seq

    wts = _inproj_weights(norm_mix_g[0], w_in[0], fox_b_f[0], fox_q_norm_g[0], fox_k_norm_g[0],
                          gla_w_gup[0], gla_b_g[0])
    w_out_l = w_out[0]
    wof_aug = jnp.zeros((FOX_HEADS, LANES, D_MODEL), F32).at[:, :FOX_HEAD_DIM].set(
        w_out_l[:FOX_DIM].reshape(FOX_HEADS, FOX_HEAD_DIM, D_MODEL)).reshape(AUG_DIM, D_MODEL).astype(BF16)
    wof = w_out_l[:FOX_DIM].astype(BF16)
    wog = w_out_l[FOX_DIM:].astype(BF16)
    w_up_b = w_up[0].astype(BF16)
    w_dn_b = w_down[0].astype(BF16)

    qt, ka, vt, k_p, v_p, lf_p, gq, gk, gv, gr, gg = _inproj(
        x_prompt, wts, t_real=t_real, prompt=True, meta=meta_tokens.astype(x_prompt.dtype))
    o_fox = _fox_prompt(qt, ka, vt)
    s0 = jnp.zeros((nb, GLA_HEADS, GLA_KEY_DIM, GLA_VAL_DIM), F32)
    o_gla, s_p = _gla(gq, gk, gv, gr, gg, s0, gla_out_norm_g[0], tm=ROW_TILE, chunk=GLA_CHUNK, seqs=nb)
    y_p = _out_mlp(x_prompt, o_fox, o_gla, wof_aug, wog, norm_mlp_g[0], w_up_b, w_dn_b,
                   fox_row_offset=N_META)

    xs = jnp.pad(x_sample, ((0, 0), (0, DECODE_ROWS - dec_seq), (0, 0))).reshape(1, n_seq * DECODE_ROWS, D_MODEL)
    qc, k_s, v_s, lf_s, hq, hk, hv, hr, hg = _inproj(xs, wts, t_real=dec_seq, prompt=False,
                                                      seq_rows=DECODE_ROWS)
    per_seq = lambda a: a.reshape(n_seq, DECODE_ROWS, a.shape[-1])
    head_of_lane = jnp.arange(FOX_DIM) // FOX_HEAD_DIM
    qmask = (head_of_lane[None, :] == jnp.arange(FOX_HEADS)[:, None]).astype(BF16)
    qbd = (per_seq(qc)[:, :, None, :] * qmask[None, None]).reshape(n_seq, DECODE_ROWS * FOX_HEADS, FOX_DIM)
    o_fox_s = _fox_decode(page_table, qbd, per_seq(k_s), per_seq(v_s), per_seq(lf_s),
                          cache_k[0], cache_v[0], cache_logf[0], t_new=dec_seq)
    o_gla_s, s_s = _gla(per_seq(hq), per_seq(hk), per_seq(hv), per_seq(hr), per_seq(hg),
                        state_gla[0].astype(F32), gla_out_norm_g[0], tm=DECODE_ROWS, chunk=DECODE_ROWS,
                        seqs=GLA_DECODE_SEQS_PER_STEP)
    y_s = _out_mlp(xs, o_fox_s.reshape(1, n_seq * DECODE_ROWS, FOX_DIM),
                   o_gla_s.reshape(1, n_seq * DECODE_ROWS, GLA_VDIM), wof, wog, norm_mlp_g[0], w_up_b, w_dn_b)

    real = lambda a: per_seq(a[0])[:, :dec_seq]
    return (y_p,
            real(y_s),
            jnp.transpose(k_p.reshape(1, nb, FOX_HEADS, FOX_HEAD_DIM, t_real), (0, 1, 4, 2, 3)),
            jnp.transpose(v_p.reshape(1, nb, FOX_HEADS, FOX_HEAD_DIM, t_real), (0, 1, 4, 2, 3)),
            jnp.transpose(lf_p, (0, 2, 1))[None],
            s_p[None],
            real(k_s).reshape(1, n_seq, dec_seq, FOX_HEADS, FOX_HEAD_DIM),
            real(v_s).reshape(1, n_seq, dec_seq, FOX_HEADS, FOX_HEAD_DIM),
            real(lf_s)[None],
            s_s[None])
```

```python
import functools

import jax
import jax.numpy as jnp
import numpy as np
from jax import lax
from jax.experimental import pallas as pl
from jax.experimental.pallas import tpu as pltpu

D_MODEL = 1024
N_META = 16
PAGE_SIZE = 128
FOX_HEADS = 8
FOX_HEAD_DIM = 64
FOX_DIM = FOX_HEADS * FOX_HEAD_DIM
GLA_HEADS = 4
GLA_KEY_DIM = 64
GLA_VAL_DIM = 128
GLA_KDIM = GLA_HEADS * GLA_KEY_DIM
GLA_VDIM = GLA_HEADS * GLA_VAL_DIM
GLA_GATE_RANK = 16
GLA_GATE_TEMP = 16.0
GLA_CHUNK = 64
D_FF = 4 * D_MODEL
EPS = 1e-6
LOG2E = 1.4426950408889634

LANES = 128
AUG_DIM = FOX_HEADS * LANES
AUG_ROWS = 16
V_ROWS = FOX_HEAD_DIM + AUG_ROWS
SMALL_COLS = LANES
ROW_TILE = 512
PROMPT_TILES_PER_TRIP = 8
PROMPT_HEADS_PER_STEP = 2
DECODE_ROWS = 8
DECODE_PAGES_PER_STEP = 32
GLA_DECODE_SEQS_PER_STEP = 8
VMEM_LIMIT = 56 * 1024 * 1024

F32 = jnp.float32
BF16 = jnp.bfloat16

_C_FQ, _C_FK, _C_FV = 0, FOX_DIM, 2 * FOX_DIM
_C_GQ = 3 * FOX_DIM
_C_GK = _C_GQ + GLA_KDIM
_C_GV = _C_GK + GLA_KDIM
_C_GR = _C_GV + GLA_VDIM
_C_SM = _C_GR + GLA_VDIM
_W_COLS = _C_SM + SMALL_COLS


def _log_sigmoid(x):
    return jnp.minimum(x, 0.0) - jnp.log1p(jnp.exp(-jnp.abs(x)))


def _split3(x):
    p1 = x.astype(BF16)
    r = x - p1.astype(F32)
    p2 = r.astype(BF16)
    r = r - p2.astype(F32)
    return p1, p2, r.astype(BF16)


def _const_spec(shape):
    return pl.BlockSpec(shape, lambda *_: (0,) * len(shape), pipeline_mode=pl.Buffered(1))


def _inproj_kernel(x_ref, gmix_ref, w_ref, gmat_ref, qg_ref, kg_ref, bf_ref, wg_ref, bg_ref, *rest,
                   tm, t_real, seq_rows, prompt):
    i = pl.program_id(1)
    if prompt:
        (meta_ref, tri_ref, pk_ref, onek_ref, selq_ref, oneq_ref, onev_ref,
         qt_ref, ka_ref, vt_ref, ko_ref, vo_ref, lf_ref,
         gq_ref, gk_ref, gv_ref, gr_ref, gg_ref, carry_ref) = rest
        n_meta = meta_ref.shape[0]
        blk = x_ref[0]
        first = jnp.concatenate([meta_ref[...], blk[0:tm - n_meta]], axis=0)
        last = jnp.concatenate([blk[tm - n_meta:tm], jnp.zeros((tm - n_meta, D_MODEL), F32)], axis=0)
        x = jnp.where(i == 0, first, jnp.where(i == pl.num_programs(1) - 1, last, blk))
    else:
        (qc_ref, ko_ref, vo_ref, lf_ref, gq_ref, gk_ref, gv_ref, gr_ref, gg_ref) = rest
        x = x_ref[0]

    ms = jnp.mean(x * x, axis=-1, keepdims=True)
    xn = (x * lax.rsqrt(ms + EPS) * gmix_ref[...]).astype(BF16)
    z = jnp.dot(xn, w_ref[...], preferred_element_type=F32)

    def head_norm(zz, g_row):
        msq = jnp.dot((zz * zz).astype(BF16), gmat_ref[...], preferred_element_type=F32)
        return zz * lax.rsqrt(msq + EPS) * g_row

    qn = head_norm(z[:, _C_FQ:_C_FQ + FOX_DIM], qg_ref[...])
    kn = head_norm(z[:, _C_FK:_C_FK + FOX_DIM], kg_ref[...])
    zv = z[:, _C_FV:_C_FV + FOX_DIM]
    zs = z[:, _C_SM:_C_SM + SMALL_COLS]
    lf = _log_sigmoid(zs + bf_ref[...])
    if prompt:
        ko_ref[0] = kn.T
        zv_t = zv.T
        vo_ref[0] = zv_t
        lf_ref[0] = lf.T[0:FOX_HEADS, :]
    else:
        ko_ref[0] = kn
        vo_ref[0] = zv
        lf_ref[0] = lf[:, 0:FOX_HEADS]

    row = i * tm + lax.broadcasted_iota(jnp.int32, (tm, 1), 0)
    if prompt:
        valid = row < t_real
    else:
        valid = (row % seq_rows) < t_real

    gq_ref[0] = z[:, _C_GQ:_C_GQ + GLA_KDIM] * (GLA_KEY_DIM ** -0.5)
    gk_ref[0] = z[:, _C_GK:_C_GK + GLA_KDIM]
    gv_ref[0] = z[:, _C_GV:_C_GV + GLA_VDIM]
    gr_ref[0] = z[:, _C_GR:_C_GR + GLA_VDIM]
    xg = jnp.dot(zs.astype(BF16), wg_ref[...], preferred_element_type=F32) + bg_ref[...]
    gg_ref[0] = jnp.where(valid, _log_sigmoid(xg) * (1.0 / GLA_GATE_TEMP), 0.0)

    if not prompt:
        qc_ref[0] = qn.astype(BF16)
        return

    @pl.when(i == 0)
    def _():
        carry_ref[...] = jnp.zeros_like(carry_ref)

    lane = lax.broadcasted_iota(jnp.int32, (tm, SMALL_COLS), 1)
    lfm = jnp.where(lane < FOX_HEADS, lf, 0.0)
    c = jnp.dot(tri_ref[...], jnp.concatenate(_split3(lfm), axis=1), preferred_element_type=F32)
    fcum = c[:, 0:LANES] + c[:, LANES:2 * LANES] + c[:, 2 * LANES:3 * LANES] + carry_ref[...]
    carry_ref[...] = fcum[tm - 1:tm, :]
    fs = fcum * LOG2E
    a1, a2, a3 = _split3(fs)

    k_cat = jnp.concatenate([kn.astype(BF16), -a1, -a2, -a3], axis=1)
    ka = jnp.dot(k_cat, pk_ref[...], preferred_element_type=F32) + onek_ref[...]
    ka_ref[0] = ka.astype(BF16)

    qn_t = qn.T
    aug_q = jnp.dot(selq_ref[...], jnp.concatenate(_split3(fs.T), axis=0),
                    preferred_element_type=F32) + oneq_ref[...]
    zero_rows = jnp.zeros((LANES - FOX_HEAD_DIM - AUG_ROWS, tm), BF16)
    for h in range(FOX_HEADS):
        hs = slice(h * FOX_HEAD_DIM, (h + 1) * FOX_HEAD_DIM)
        qt_ref[0, h, 0:FOX_HEAD_DIM, :] = qn_t[hs].astype(BF16)
        qt_ref[0, h, FOX_HEAD_DIM:FOX_HEAD_DIM + AUG_ROWS, :] = aug_q[h * AUG_ROWS:(h + 1) * AUG_ROWS].astype(BF16)
        qt_ref[0, h, FOX_HEAD_DIM + AUG_ROWS:LANES, :] = zero_rows
        vt_ref[0, h, 0:FOX_HEAD_DIM, :] = zv_t[hs].astype(BF16)
        vt_ref[0, h, FOX_HEAD_DIM:FOX_HEAD_DIM + AUG_ROWS, :] = jnp.broadcast_to(
            onev_ref[...], (AUG_ROWS, tm)).astype(BF16)


def _placement_constants():
    n_cat = FOX_DIM + 3 * LANES
    pk = np.zeros((n_cat, AUG_DIM), np.float32)
    onek = np.zeros((1, AUG_DIM), np.float32)
    selq = np.zeros((FOX_HEADS * AUG_ROWS, 3 * LANES), np.float32)
    oneq = np.zeros((FOX_HEADS * AUG_ROWS, 1), np.float32)
    onev = np.zeros((AUG_ROWS, 1), np.float32)
    onev[0, 0] = 1.0
    for h in range(FOX_HEADS):
        base = h * LANES
        for d in range(FOX_HEAD_DIM):
            pk[h * FOX_HEAD_DIM + d, base + d] = 1.0
        for piece in range(3):
            pk[FOX_DIM + piece * LANES + h, base + FOX_HEAD_DIM + piece] = 1.0
            onek[0, base + FOX_HEAD_DIM + 3 + piece] = 1.0
            oneq[h * AUG_ROWS + piece, 0] = 1.0
            selq[h * AUG_ROWS + 3 + piece, piece * LANES + h] = 1.0
    gmat = np.kron(np.eye(FOX_HEADS, dtype=np.float32),
                   np.full((FOX_HEAD_DIM, FOX_HEAD_DIM), 1.0 / FOX_HEAD_DIM, np.float32))
    return pk, onek, selq, oneq, onev, gmat


def _inproj_weights(norm_mix_g, w_in, fox_b_f, fox_q_norm_g, fox_k_norm_g, gla_w_gup, gla_b_g):
    o = 0
    cols = {}
    for name, width in (("fq", FOX_DIM), ("fk", FOX_DIM), ("fv", FOX_DIM), ("ff", FOX_HEADS),
                        ("gq", GLA_KDIM), ("gk", GLA_KDIM), ("gv", GLA_VDIM), ("gr", GLA_VDIM),
                        ("glr", GLA_GATE_RANK)):
        cols[name] = w_in[:, o:o + width]
        o += width
    pad = jnp.zeros((D_MODEL, SMALL_COLS - FOX_HEADS - GLA_GATE_RANK), w_in.dtype)
    w = jnp.concatenate([cols["fq"], cols["fk"], cols["fv"], cols["gq"], cols["gk"], cols["gv"],
                         cols["gr"], cols["ff"], cols["glr"], pad], axis=1).astype(BF16)
    wg = jnp.zeros((SMALL_COLS, GLA_KDIM), F32).at[FOX_HEADS:FOX_HEADS + GLA_GATE_RANK].set(gla_w_gup)
    bf = jnp.zeros((1, SMALL_COLS), F32).at[0, :FOX_HEADS].set(fox_b_f)
    qg = jnp.tile(fox_q_norm_g, FOX_HEADS)[None, :] * (FOX_HEAD_DIM ** -0.5 * LOG2E)
    kg = jnp.tile(fox_k_norm_g, FOX_HEADS)[None, :]
    return dict(gmix=norm_mix_g[None, :], w=w, qg=qg, kg=kg, bf=bf, wg=wg.astype(BF16),
                bg=gla_b_g[None, :])


def _inproj(x, wts, *, t_real, prompt, seq_rows=0, meta=None):
    nb, x_rows, _ = x.shape
    tm = ROW_TILE
    row_spec = lambda width: pl.BlockSpec((1, tm, width), lambda b, i: (b, i, 0))
    if prompt:
        n_meta = meta.shape[0]
        assert x_rows % tm == 0 and t_real == n_meta + x_rows and n_meta % 16 == 0 and n_meta < tm
        rows = x_rows + tm
        x_spec = pl.BlockSpec(
            (pl.Element(1), pl.Element(tm), pl.Element(D_MODEL)),
            lambda b, i: (b, pl.multiple_of(jnp.clip(i * tm - n_meta, 0, x_rows - tm), 16), 0))
    else:
        rows = x_rows
        x_spec = row_spec(D_MODEL)
    nt = rows // tm
    pk, onek, selq, oneq, onev, gmat = _placement_constants()
    in_specs = [x_spec, _const_spec((1, D_MODEL)), _const_spec((D_MODEL, _W_COLS)),
                _const_spec((FOX_DIM, FOX_DIM)), _const_spec((1, FOX_DIM)), _const_spec((1, FOX_DIM)),
                _const_spec((1, SMALL_COLS)), _const_spec((SMALL_COLS, GLA_KDIM)),
                _const_spec((1, GLA_KDIM))]
    args = [x, wts["gmix"], wts["w"], jnp.asarray(gmat, BF16), wts["qg"], wts["kg"], wts["bf"],
            wts["wg"], wts["bg"]]
    gla_shapes = [jax.ShapeDtypeStruct((nb, rows, GLA_KDIM), F32),
                  jax.ShapeDtypeStruct((nb, rows, GLA_KDIM), F32),
                  jax.ShapeDtypeStruct((nb, rows, GLA_VDIM), F32),
                  jax.ShapeDtypeStruct((nb, rows, GLA_VDIM), F32),
                  jax.ShapeDtypeStruct((nb, rows, GLA_KDIM), F32)]
    gla_specs = [row_spec(GLA_KDIM), row_spec(GLA_KDIM), row_spec(GLA_VDIM), row_spec(GLA_VDIM),
                 row_spec(GLA_KDIM)]
    scratch = []
    if prompt:
        n_cat = FOX_DIM + 3 * LANES
        tri = np.tril(np.ones((tm, tm), np.float32))
        in_specs += [_const_spec((n_meta, D_MODEL)),
                     _const_spec((tm, tm)), _const_spec((n_cat, AUG_DIM)), _const_spec((1, AUG_DIM)),
                     _const_spec((FOX_HEADS * AUG_ROWS, 3 * LANES)), _const_spec((FOX_HEADS * AUG_ROWS, 1)),
                     _const_spec((AUG_ROWS, 1))]
        args += [meta, jnp.asarray(tri, BF16), jnp.asarray(pk, BF16), jnp.asarray(onek),
                 jnp.asarray(selq, BF16), jnp.asarray(oneq), jnp.asarray(onev)]
        head_spec = lambda height: pl.BlockSpec((1, FOX_HEADS, height, tm), lambda b, i: (b, 0, 0, i))
        out_shape = [jax.ShapeDtypeStruct((nb, FOX_HEADS, LANES, rows), BF16),
                     jax.ShapeDtypeStruct((nb, rows, AUG_DIM), BF16),
                     jax.ShapeDtypeStruct((nb, FOX_HEADS, V_ROWS, rows), BF16)]
        out_specs = [head_spec(LANES), row_spec(AUG_DIM), head_spec(V_ROWS)]
        col_spec = lambda height: pl.BlockSpec((1, height, tm), lambda b, i: (b, 0, i))
        out_shape += [jax.ShapeDtypeStruct((nb, FOX_DIM, t_real), F32),
                      jax.ShapeDtypeStruct((nb, FOX_DIM, t_real), F32),
                      jax.ShapeDtypeStruct((nb, FOX_HEADS, t_real), F32)]
        out_specs += [col_spec(FOX_DIM), col_spec(FOX_DIM), col_spec(FOX_HEADS)]
        scratch = [pltpu.VMEM((1, SMALL_COLS), F32)]
    else:
        out_shape = [jax.ShapeDtypeStruct((nb, rows, FOX_DIM), BF16),
                     jax.ShapeDtypeStruct((nb, rows, FOX_DIM), F32),
                     jax.ShapeDtypeStruct((nb, rows, FOX_DIM), F32),
                     jax.ShapeDtypeStruct((nb, rows, FOX_HEADS), F32)]
        out_specs = [row_spec(FOX_DIM), row_spec(FOX_DIM), row_spec(FOX_DIM), row_spec(FOX_HEADS)]
    out_shape += gla_shapes
    out_specs += gla_specs
    return pl.pallas_call(
        functools.partial(_inproj_kernel, tm=tm, t_real=t_real, seq_rows=seq_rows, prompt=prompt),
        grid=(nb, nt),
        in_specs=in_specs,
        out_specs=out_specs,
        out_shape=out_shape,
        scratch_shapes=scratch,
        compiler_params=pltpu.CompilerParams(dimension_semantics=("arbitrary", "arbitrary"),
                                             vmem_limit_bytes=VMEM_LIMIT),
        name="inproj_prompt" if prompt else "inproj_decode",
    )(*args)


def _fox_prompt_kernel(qt_ref, k_ref, vt_ref, o_ref, *scratch, tile, heads):
    i = pl.program_id(2)
    per_head = [scratch[6 * hs:6 * hs + 6] for hs in range(heads)]
    for _, _, _, _, m_ref, acc_ref in per_head:
        m_ref[...] = jnp.full_like(m_ref, -jnp.inf)
        acc_ref[...] = jnp.zeros_like(acc_ref)

    def scores(t, parity, causal):
        for hs in range(heads):
            s_ref, mt_ref = per_head[hs][parity], per_head[hs][2 + parity]
            k = k_ref[0, pl.ds(pl.multiple_of(t * tile, tile), tile), hs * LANES:(hs + 1) * LANES]
            s = jnp.dot(k, qt_ref[0, hs], preferred_element_type=F32)
            if causal:
                r = lax.broadcasted_iota(jnp.int32, (tile, tile), 0)
                c = lax.broadcasted_iota(jnp.int32, (tile, tile), 1)
                s = jnp.where(r <= c, s, -jnp.inf)
            s_ref[...] = s
            mt_ref[...] = jnp.max(s, axis=0, keepdims=True)

    def accumulate(t, parity):
        for hs in range(heads):
            s_ref, mt_ref = per_head[hs][parity], per_head[hs][2 + parity]
            m_ref, acc_ref = per_head[hs][4], per_head[hs][5]
            vt = vt_ref[0, hs, :, pl.ds(pl.multiple_of(t * tile, tile), tile)]
            m_old = m_ref[...]
            m_new = jnp.maximum(m_old, mt_ref[...])
            p = jnp.exp2(s_ref[...] - m_new)
            acc_ref[...] = acc_ref[...] * jnp.exp2(m_old - m_new) + jnp.dot(
                vt, p.astype(BF16), preferred_element_type=F32)
            m_ref[...] = m_new

    def step(t, parity, causal_next):
        scores(t + 1, 1 - parity, causal_next)
        accumulate(t, parity)

    odd = (i % 2) == 1

    @pl.when(i == 0)
    def _():
        scores(0, 0, True)
        accumulate(0, 0)

    @pl.when(i > 0)
    def _():
        scores(0, 0, False)
        unroll = PROMPT_TILES_PER_TRIP
        n_trips = (i - 1) // unroll

        def trip(jj, carry):
            for u in range(unroll):
                step(unroll * jj + u, u % 2, False)
            return carry

        lax.fori_loop(0, n_trips, trip, 0)
        t0 = unroll * n_trips
        pre = i - t0 - jnp.where(odd, 1, 2)

        @pl.when(pre >= 4)
        def _():
            for u in range(4):
                step(t0 + u, u % 2, False)

        t1 = t0 + jnp.where(pre >= 4, 4, 0)

        @pl.when(pre % 4 == 2)
        def _():
            step(t1, 0, False)
            step(t1 + 1, 1, False)

        @pl.when(odd)
        def _():
            step(i - 1, 0, True)
            accumulate(i, 1)

        @pl.when(jnp.logical_not(odd))
        def _():
            step(i - 2, 0, False)
            step(i - 1, 1, True)
            accumulate(i, 0)

    for hs in range(heads):
        acc = per_head[hs][5][...]
        o_t = acc[0:FOX_HEAD_DIM] / acc[FOX_HEAD_DIM:FOX_HEAD_DIM + 1]
        o_t = jnp.concatenate([o_t, jnp.zeros((LANES - FOX_HEAD_DIM, tile), F32)], axis=0)
        o_ref[0, :, hs * LANES:(hs + 1) * LANES] = o_t.T.astype(BF16)


def _fox_prompt(qt, ka, vt):
    nb, rows, _ = ka.shape
    tile = ROW_TILE
    nq = rows // tile
    hps = PROMPT_HEADS_PER_STEP
    per_head_scratch = [pltpu.VMEM((tile, tile), F32), pltpu.VMEM((tile, tile), F32),
                        pltpu.VMEM((1, tile), F32), pltpu.VMEM((1, tile), F32),
                        pltpu.VMEM((1, tile), F32), pltpu.VMEM((V_ROWS, tile), F32)]
    return pl.pallas_call(
        functools.partial(_fox_prompt_kernel, tile=tile, heads=hps),
        grid=(nb, FOX_HEADS // hps, nq),
        in_specs=[pl.BlockSpec((1, hps, LANES, tile), lambda b, h, i: (b, h, 0, i)),
                  pl.BlockSpec((1, rows, hps * LANES), lambda b, h, i: (b, 0, h)),
                  pl.BlockSpec((1, hps, V_ROWS, rows), lambda b, h, i: (b, h, 0, 0))],
        out_specs=pl.BlockSpec((1, tile, hps * LANES), lambda b, h, i: (b, i, h)),
        out_shape=jax.ShapeDtypeStruct((nb, rows, AUG_DIM), BF16),
        scratch_shapes=per_head_scratch * hps,
        compiler_params=pltpu.CompilerParams(
            dimension_semantics=("arbitrary", "arbitrary", "arbitrary"), vmem_limit_bytes=VMEM_LIMIT),
        name="fox_prompt",
    )(qt, ka, vt)


def _fox_decode_kernel(pt_ref, qbd_ref, kn_ref, vn_ref, lfn_ref, usuf_ref, *rest, n_pages, t_new):
    del pt_ref
    p = n_pages
    k_refs, v_refs, lf_refs = rest[0:p], rest[p:2 * p], rest[2 * p:3 * p]
    o_ref, m_ref, l_ref, acc_ref, carry_ref, qoff_ref = rest[3 * p:]
    j = pl.program_id(1)
    nq = DECODE_ROWS * FOX_HEADS
    qbd = qbd_ref[0]

    def suffix_sums(lf, token_minor):
        n = lf.shape[1] if token_minor else lf.shape[0]
        u = usuf_ref[0:n, :]
        u = jnp.concatenate([u[:, 0:n], u[:, PAGE_SIZE:PAGE_SIZE + LANES]], axis=1)
        if token_minor:
            r = jnp.dot(jnp.concatenate(_split3(lf), axis=0), u, preferred_element_type=F32)
        else:
            r = lax.dot_general(jnp.concatenate(_split3(lf), axis=1), u, (((0,), (0,)), ((), ())),
                                preferred_element_type=F32)
        r = r[0:FOX_HEADS] + r[FOX_HEADS:2 * FOX_HEADS] + r[2 * FOX_HEADS:3 * FOX_HEADS]
        return r[:, 0:n], r[:, n:n + LANES]

    def attend(k, v, bias_t, causal, token_minor):
        k = k.astype(BF16)
        v = v.astype(BF16)
        if token_minor:
            n = k.shape[1]
            s = jnp.dot(qbd, k, preferred_element_type=F32)
        else:
            n = k.shape[0]
            s = lax.dot_general(qbd, k, (((1,), (1,)), ((), ())), preferred_element_type=F32)
        s = s + jnp.concatenate([bias_t] * DECODE_ROWS, axis=0) - qoff_ref[...]
        if causal:
            r = lax.broadcasted_iota(jnp.int32, (nq, n), 0)
            c = lax.broadcasted_iota(jnp.int32, (nq, n), 1)
            s = jnp.where(c * FOX_HEADS <= r, s, -jnp.inf)
        m_old = m_ref[...]
        m_new = jnp.maximum(m_old, jnp.max(s, axis=-1, keepdims=True))
        alpha = jnp.exp2(m_old - m_new)
        pr = jnp.exp2(s - m_new)
        l_ref[...] = l_ref[...] * alpha + jnp.sum(pr, axis=-1, keepdims=True)
        pv_dims = (((1,), (1,)), ((), ())) if token_minor else (((1,), (0,)), ((), ()))
        acc_ref[...] = acc_ref[...] * alpha + lax.dot_general(pr.astype(BF16), v, pv_dims,
                                                              preferred_element_type=F32)
        m_ref[...] = m_new

    @pl.when(j == 0)
    def _():
        m_ref[...] = jnp.full_like(m_ref, -jnp.inf)
        l_ref[...] = jnp.zeros_like(l_ref)
        acc_ref[...] = jnp.zeros_like(acc_ref)
        rowi = lax.broadcasted_iota(jnp.int32, (DECODE_ROWS, FOX_HEADS), 0)
        lf = jnp.where(rowi < t_new, lfn_ref[0], 0.0) * LOG2E
        r_t, tot = suffix_sums(lf, False)
        rt_rows = jnp.concatenate([r_t] * DECODE_ROWS, axis=0)
        r = lax.broadcasted_iota(jnp.int32, (nq, DECODE_ROWS), 0)
        c = lax.broadcasted_iota(jnp.int32, (nq, DECODE_ROWS), 1)
        own_t = jnp.logical_and(c * FOX_HEADS <= r, r < (c + 1) * FOX_HEADS)
        qoff_ref[...] = jnp.sum(jnp.where(own_t, rt_rows, 0.0), axis=-1, keepdims=True)
        attend(kn_ref[0], vn_ref[0], r_t, True, False)
        carry_ref[...] = tot

    n_piece_rows = 3 * FOX_HEADS
    lf_stack = jnp.concatenate(
        [piece for idx in range(p) for piece in _split3(lf_refs[idx][...] * LOG2E)], axis=0)
    r_all = jnp.dot(lf_stack, usuf_ref[...], preferred_element_type=F32)
    carry = carry_ref[...]
    biases = []
    for idx in range(p):
        r = r_all[idx * n_piece_rows:(idx + 1) * n_piece_rows]
        r = r[0:FOX_HEADS] + r[FOX_HEADS:2 * FOX_HEADS] + r[2 * FOX_HEADS:3 * FOX_HEADS]
        biases.append(r[:, 0:PAGE_SIZE] + carry)
        carry = carry + r[:, PAGE_SIZE:PAGE_SIZE + LANES]
    carry_ref[...] = carry
    attend(jnp.concatenate([k_refs[idx][...].astype(BF16) for idx in range(p)], axis=1),
           jnp.concatenate([v_refs[idx][...].astype(BF16) for idx in range(p)], axis=1),
           jnp.concatenate(biases, axis=1), False, True)

    @pl.when(j == pl.num_programs(1) - 1)
    def _():
        o = acc_ref[...] / l_ref[...]
        r = lax.broadcasted_iota(jnp.int32, (nq, FOX_DIM), 0)
        c = lax.broadcasted_iota(jnp.int32, (nq, FOX_DIM), 1)
        own = (c // FOX_HEAD_DIM) == (r % FOX_HEADS)
        o = jnp.where(own, o, 0.0).reshape(DECODE_ROWS, FOX_HEADS, FOX_DIM)
        o_ref[0] = jnp.sum(o, axis=1).astype(BF16)


def _fox_decode(page_table, qbd, k_new, v_new, lf_new, cache_k, cache_v, cache_logf, *, t_new):
    n_seq, n_pages_seq = page_table.shape
    p = DECODE_PAGES_PER_STEP
    n_steps = n_pages_seq // p
    nq = DECODE_ROWS * FOX_HEADS
    n_pool = cache_k.shape[0]
    ck = jnp.transpose(cache_k, (0, 2, 3, 1)).reshape(n_pool, FOX_DIM, PAGE_SIZE)
    cv = jnp.transpose(cache_v, (0, 2, 3, 1)).reshape(n_pool, FOX_DIM, PAGE_SIZE)
    clf = jnp.transpose(cache_logf, (0, 2, 1))
    usuf = np.concatenate([np.tril(np.ones((PAGE_SIZE, PAGE_SIZE), np.float32), -1),
                           np.ones((PAGE_SIZE, LANES), np.float32)], axis=1)

    def page_map(idx):
        return lambda n, j, pt: (pt[n * n_pages_seq + n_pages_seq - 1 - (j * p + idx)], 0, 0)

    seq_spec = lambda r, w: pl.BlockSpec((1, r, w), lambda n, j, pt: (n, 0, 0))
    in_specs = [seq_spec(nq, FOX_DIM), seq_spec(DECODE_ROWS, FOX_DIM), seq_spec(DECODE_ROWS, FOX_DIM),
                seq_spec(DECODE_ROWS, FOX_HEADS),
                pl.BlockSpec((PAGE_SIZE, PAGE_SIZE + LANES), lambda n, j, pt: (0, 0))]
    in_specs += [pl.BlockSpec((None, FOX_DIM, PAGE_SIZE), page_map(idx)) for idx in range(p)]
    in_specs += [pl.BlockSpec((None, FOX_DIM, PAGE_SIZE), page_map(idx)) for idx in range(p)]
    in_specs += [pl.BlockSpec((None, FOX_HEADS, PAGE_SIZE), page_map(idx)) for idx in range(p)]
    grid_spec = pltpu.PrefetchScalarGridSpec(
        num_scalar_prefetch=1,
        grid=(n_seq, n_steps),
        in_specs=in_specs,
        out_specs=pl.BlockSpec((1, DECODE_ROWS, FOX_DIM), lambda n, j, pt: (n, 0, 0)),
        scratch_shapes=[pltpu.VMEM((nq, 1), F32), pltpu.VMEM((nq, 1), F32), pltpu.VMEM((nq, FOX_DIM), F32),
                        pltpu.VMEM((FOX_HEADS, LANES), F32), pltpu.VMEM((nq, 1), F32)],
    )
    return pl.pallas_call(
        functools.partial(_fox_decode_kernel, n_pages=p, t_new=t_new),
        grid_spec=grid_spec,
        out_shape=jax.ShapeDtypeStruct((n_seq, DECODE_ROWS, FOX_DIM), BF16),
        compiler_params=pltpu.CompilerParams(dimension_semantics=("arbitrary", "arbitrary"),
                                             vmem_limit_bytes=VMEM_LIMIT),
        name="fox_decode",
    )(page_table.reshape(-1), qbd, k_new, v_new, lf_new, jnp.asarray(usuf, BF16),
      *([ck] * p), *([cv] * p), *([clf] * p))


def _gla_kernel(gq_ref, gk_ref, gv_ref, gr_ref, gg_ref, s0_ref, gn_ref, tri_ref,
                o_ref, sfin_ref, st_ref, *, tm, chunk, seqs):
    i = pl.program_id(1)
    n_pairs = GLA_HEADS // 2
    pair = 2 * GLA_KEY_DIM

    @pl.when(i == 0)
    def _():
        for sq in range(seqs):
            for hp in range(n_pairs):
                st_ref[sq, hp] = s0_ref[sq, 2 * hp:2 * hp + 2].reshape(pair, GLA_VAL_DIM).T

    nch = tm // chunk
    lane = lax.broadcasted_iota(jnp.int32, (tm, pair), 1)
    rr = lax.broadcasted_iota(jnp.int32, (tm, tm), 0)
    cc = lax.broadcasted_iota(jnp.int32, (tm, tm), 1)
    in_chunk = jnp.logical_and(cc <= rr, cc >= rr - rr % chunk)
    tri = tri_ref[...]
    gn = gn_ref[...]
    by_chunk = lambda a: a.reshape(nch, chunk, a.shape[-1])

    for sq, hp in [(a, b) for a in range(seqs) for b in range(n_pairs)]:
        ks = slice(hp * pair, (hp + 1) * pair)
        g = gg_ref[sq, :, ks]
        q = gq_ref[sq, :, ks]
        k = gk_ref[sq, :, ks]
        cb = jnp.dot(tri, jnp.concatenate(_split3(g), axis=1), preferred_element_type=F32)
        bc = cb[:, 0:pair] + cb[:, pair:2 * pair] + cb[:, 2 * pair:3 * pair]
        bc3 = by_chunk(bc)
        b_last = bc3[:, chunk - 1:chunk, :]
        qt = q * jnp.exp(bc)
        kt = (k * jnp.exp(-bc)).astype(BF16)
        kh = (by_chunk(k) * jnp.exp(b_last - bc3)).reshape(tm, pair)
        decay = jnp.exp(b_last)
        heads = []
        upd = None
        for hh in range(2):
            own = (lane // GLA_KEY_DIM) == hh
            vs = slice((2 * hp + hh) * GLA_VAL_DIM, (2 * hp + hh + 1) * GLA_VAL_DIM)
            v = gv_ref[sq, :, vs].astype(BF16)
            qm = jnp.where(own, qt, 0.0).astype(BF16)
            khm = jnp.where(own, kh, 0.0).astype(BF16)
            u = lax.dot_general(by_chunk(v), by_chunk(khm), (((1,), (1,)), ((0,), (0,))),
                                preferred_element_type=F32)
            upd = u if upd is None else upd + u
            heads.append((vs, v, qm))
        st = st_ref[sq, hp]
        entering = []
        for c in range(nch):
            entering.append(st)
            st = st * decay[c] + upd[c]
        st_ref[sq, hp] = st
        st_in = jnp.stack(entering).astype(BF16)
        for vs, v, qm in heads:
            a = lax.dot_general(qm, kt, (((1,), (1,)), ((), ())), preferred_element_type=F32)
            a = jnp.where(in_chunk, a, 0.0).astype(BF16)
            o = jnp.dot(a, v, preferred_element_type=F32)
            o = o + lax.dot_general(by_chunk(qm), st_in, (((2,), (2,)), ((0,), (0,))),
                                    preferred_element_type=F32).reshape(tm, GLA_VAL_DIM)
            on = o * lax.rsqrt(jnp.mean(o * o, axis=-1, keepdims=True) + EPS) * gn
            gate = gr_ref[sq, :, vs]
            gate = gate * (1.0 / (1.0 + jnp.exp(-gate)))
            o_ref[sq, :, vs] = (on * gate).astype(BF16)

    @pl.when(i == pl.num_programs(1) - 1)
    def _():
        for sq in range(seqs):
            for hp in range(n_pairs):
                sfin_ref[sq, 2 * hp:2 * hp + 2] = st_ref[sq, hp].T.reshape(2, GLA_KEY_DIM, GLA_VAL_DIM)


def _gla(gq, gk, gv, gr, gg, s0, gla_out_norm_g, *, tm, chunk, seqs):
    nb, rows, _ = gq.shape
    nt = rows // tm
    assert nb % seqs == 0
    tri = np.kron(np.eye(tm // chunk, dtype=np.float32), np.tril(np.ones((chunk, chunk), np.float32)))
    row_spec = lambda width: pl.BlockSpec((seqs, tm, width), lambda b, i: (b, i, 0))
    state_spec = pl.BlockSpec((seqs, GLA_HEADS, GLA_KEY_DIM, GLA_VAL_DIM), lambda b, i: (b, 0, 0, 0))
    return pl.pallas_call(
        functools.partial(_gla_kernel, tm=tm, chunk=chunk, seqs=seqs),
        grid=(nb // seqs, nt),
        in_specs=[row_spec(GLA_KDIM), row_spec(GLA_KDIM), row_spec(GLA_VDIM), row_spec(GLA_VDIM),
                  row_spec(GLA_KDIM), state_spec,
                  pl.BlockSpec((1, GLA_VAL_DIM), lambda b, i: (0, 0)),
                  pl.BlockSpec((tm, tm), lambda b, i: (0, 0))],
        out_specs=[row_spec(GLA_VDIM), state_spec],
        out_shape=[jax.ShapeDtypeStruct((nb, rows, GLA_VDIM), BF16),
                   jax.ShapeDtypeStruct((nb, GLA_HEADS, GLA_KEY_DIM, GLA_VAL_DIM), F32)],
        scratch_shapes=[pltpu.VMEM((seqs, GLA_HEADS // 2, GLA_VAL_DIM, 2 * GLA_KEY_DIM), F32)],
        compiler_params=pltpu.CompilerParams(dimension_semantics=("arbitrary", "arbitrary"),
                                             vmem_limit_bytes=VMEM_LIMIT),
        name="gla",
    )(gq, gk, gv, gr, gg, s0, gla_out_norm_g[None, :], jnp.asarray(tri, BF16))


def _out_mlp_kernel(x_ref, of_ref, og_ref, wof_ref, wog_ref, gm_ref, wup_ref, wdn_ref, y_ref, *, ff_chunk):
    x = x_ref[0]
    h = (x + jnp.dot(of_ref[0], wof_ref[...], preferred_element_type=F32)
         + jnp.dot(og_ref[0], wog_ref[...], preferred_element_type=F32))
    ms = jnp.mean(h * h, axis=-1, keepdims=True)
    hn = (h * lax.rsqrt(ms + EPS) * gm_ref[...]).astype(BF16)
    y_ref[0] = h
    for c in range(D_FF // ff_chunk):
        cs = slice(c * ff_chunk, (c + 1) * ff_chunk)
        u = jnp.maximum(jnp.dot(hn, wup_ref[:, cs], preferred_element_type=F32), 0.0)
        y_ref[0] += jnp.dot((u * u).astype(BF16), wdn_ref[cs, :], preferred_element_type=F32)


def _out_mlp(x, o_fox, o_gla, wof, wog, norm_mlp_g, w_up, w_down, *, fox_row_offset=0):
    nb, rows, _ = x.shape
    tm = ROW_TILE
    nt = rows // tm
    kf = o_fox.shape[-1]
    if fox_row_offset:
        assert fox_row_offset % 16 == 0, "row offset must keep bf16 sublane tiles aligned"
        shifted = lambda b, i: (b, pl.multiple_of(i * tm + fox_row_offset, 16), 0)
        of_spec = pl.BlockSpec((pl.Element(1), pl.Element(tm), pl.Element(kf)), shifted)
        og_spec = pl.BlockSpec((pl.Element(1), pl.Element(tm), pl.Element(GLA_VDIM)), shifted)
    else:
        of_spec = pl.BlockSpec((1, tm, kf), lambda b, i: (b, i, 0))
        og_spec = pl.BlockSpec((1, tm, GLA_VDIM), lambda b, i: (b, i, 0))
    return pl.pallas_call(
        functools.partial(_out_mlp_kernel, ff_chunk=D_MODEL),
        grid=(nb, nt),
        in_specs=[pl.BlockSpec((1, tm, D_MODEL), lambda b, i: (b, i, 0)), of_spec, og_spec,
                  _const_spec((kf, D_MODEL)), _const_spec((GLA_VDIM, D_MODEL)), _const_spec((1, D_MODEL)),
                  _const_spec((D_MODEL, D_FF)), _const_spec((D_FF, D_MODEL))],
        out_specs=pl.BlockSpec((1, tm, D_MODEL), lambda b, i: (b, i, 0)),
        out_shape=jax.ShapeDtypeStruct((nb, rows, D_MODEL), F32),
        compiler_params=pltpu.CompilerParams(dimension_semantics=("arbitrary", "arbitrary"),
                                             vmem_limit_bytes=VMEM_LIMIT),
        name="out_mlp",
    )(x, o_fox, o_gla, wof, wog, norm_mlp_g[None, :], w_up, w_down)


def kernel(x_prompt, x_sample, cache_k, cache_v, cache_logf, state_gla, page_table, meta_tokens, norm_mix_g,
           w_in, fox_b_f, fox_q_norm_g, fox_k_norm_g, gla_w_gup, gla_b_g, gla_out_norm_g, w_out, norm_mlp_g,
           w_up, w_down):
    depth = w_in.shape[0]
    assert depth == 1, "single-layer trunk"
    nb, seq, _ = x_prompt.shape
    n_seq, dec_seq, _ = x_sample.shape
    t_real = N_META + seq

    wts = _inproj_weights(norm_mix_g[0], w_in[0], fox_b_f[0], fox_q_norm_g[0], fox_k_norm_g[0],
                          gla_w_gup[0], gla_b_g[0])
    w_out_l = w_out[0]
    wof_aug = jnp.zeros((FOX_HEADS, LANES, D_MODEL), F32).at[:, :FOX_HEAD_DIM].set(
        w_out_l[:FOX_DIM].reshape(FOX_HEADS, FOX_HEAD_DIM, D_MODEL)).reshape(AUG_DIM, D_MODEL).astype(BF16)
    wof = w_out_l[:FOX_DIM].astype(BF16)
    wog = w_out_l[FOX_DIM:].astype(BF16)
    w_up_b = w_up[0].astype(BF16)
    w_dn_b = w_down[0].astype(BF16)

    qt, ka, vt, k_p, v_p, lf_p, gq, gk, gv, gr, gg = _inproj(
        x_prompt, wts, t_real=t_real, prompt=True, meta=meta_tokens.astype(x_prompt.dtype))
    o_fox = _fox_prompt(qt, ka, vt)
    s0 = jnp.zeros((nb, GLA_HEADS, GLA_KEY_DIM, GLA_VAL_DIM), F32)
    o_gla, s_p = _gla(gq, gk, gv, gr, gg, s0, gla_out_norm_g[0], tm=ROW_TILE, chunk=GLA_CHUNK, seqs=nb)
    y_p = _out_mlp(x_prompt, o_fox, o_gla, wof_aug, wog, norm_mlp_g[0], w_up_b, w_dn_b,
                   fox_row_offset=N_META)

    xs = jnp.pad(x_sample, ((0, 0), (0, DECODE_ROWS - dec_seq), (0, 0))).reshape(1, n_seq * DECODE_ROWS, D_MODEL)
    qc, k_s, v_s, lf_s, hq, hk, hv, hr, hg = _inproj(xs, wts, t_real=dec_seq, prompt=False,
                                                      seq_rows=DECODE_ROWS)
    per_seq = lambda a: a.reshape(n_seq, DECODE_ROWS, a.shape[-1])
    head_of_lane = jnp.arange(FOX_DIM) // FOX_HEAD_DIM
    qmask = (head_of_lane[None, :] == jnp.arange(FOX_HEADS)[:, None]).astype(BF16)
    qbd = (per_seq(qc)[:, :, None, :] * qmask[None, None]).reshape(n_seq, DECODE_ROWS * FOX_HEADS, FOX_DIM)
    o_fox_s = _fox_decode(page_table, qbd, per_seq(k_s), per_seq(v_s), per_seq(lf_s),
                          cache_k[0], cache_v[0], cache_logf[0], t_new=dec_seq)
    o_gla_s, s_s = _gla(per_seq(hq), per_seq(hk), per_seq(hv), per_seq(hr), per_seq(hg),
                        state_gla[0].astype(F32), gla_out_norm_g[0], tm=DECODE_ROWS, chunk=DECODE_ROWS,
                        seqs=GLA_DECODE_SEQS_PER_STEP)
    y_s = _out_mlp(xs, o_fox_s.reshape(1, n_seq * DECODE_ROWS, FOX_DIM),
                   o_gla_s.reshape(1, n_seq * DECODE_ROWS, GLA_VDIM), wof, wog, norm_mlp_g[0], w_up_b, w_dn_b)

    real = lambda a: per_seq(a[0])[:, :dec_seq]
    return (y_p,
            real(y_s),
            jnp.transpose(k_p.reshape(1, nb, FOX_HEADS, FOX_HEAD_DIM, t_real), (0, 1, 4, 2, 3)),
            jnp.transpose(v_p.reshape(1, nb, FOX_HEADS, FOX_HEAD_DIM, t_real), (0, 1, 4, 2, 3)),
            jnp.transpose(lf_p, (0, 2, 1))[None],
            s_p[None],
            real(k_s).reshape(1, n_seq, dec_seq, FOX_HEADS, FOX_HEAD_DIM),
            real(v_s).reshape(1, n_seq, dec_seq, FOX_HEADS, FOX_HEAD_DIM),
            real(lf_s)[None],
            s_s[None])
```

```python
import functools

import jax
import jax.numpy as jnp
import numpy as np
from jax import lax
from jax.experimental import pallas as pl
from jax.experimental.pallas import tpu as pltpu

D_MODEL = 1024
N_META = 16
PAGE_SIZE = 128
FOX_HEADS = 8
FOX_HEAD_DIM = 64
FOX_DIM = FOX_HEADS * FOX_HEAD_DIM
GLA_HEADS = 4
GLA_KEY_DIM = 64
GLA_VAL_DIM = 128
GLA_KDIM = GLA_HEADS * GLA_KEY_DIM
GLA_VDIM = GLA_HEADS * GLA_VAL_DIM
GLA_GATE_RANK = 16
GLA_GATE_TEMP = 16.0
GLA_CHUNK = 64
D_FF = 4 * D_MODEL
EPS = 1e-6
LOG2E = 1.4426950408889634

LANES = 128
AUG_DIM = FOX_HEADS * LANES
AUG_ROWS = 16
V_ROWS = FOX_HEAD_DIM + AUG_ROWS
SMALL_COLS = LANES
ROW_TILE = 512
PROMPT_TILES_PER_TRIP = 8
PROMPT_HEADS_PER_STEP = 2
DECODE_ROWS = 8
DECODE_PAGES_PER_STEP = 32
GLA_DECODE_SEQS_PER_STEP = 8
VMEM_LIMIT = 56 * 1024 * 1024

F32 = jnp.float32
BF16 = jnp.bfloat16

_C_FQ, _C_FK, _C_FV = 0, FOX_DIM, 2 * FOX_DIM
_C_GQ = 3 * FOX_DIM
_C_GK = _C_GQ + GLA_KDIM
_C_GV = _C_GK + GLA_KDIM
_C_GR = _C_GV + GLA_VDIM
_C_SM = _C_GR + GLA_VDIM
_W_COLS = _C_SM + SMALL_COLS


def _log_sigmoid(x):
    return jnp.minimum(x, 0.0) - jnp.log1p(jnp.exp(-jnp.abs(x)))


def _split3(x):
    p1 = x.astype(BF16)
    r = x - p1.astype(F32)
    p2 = r.astype(BF16)
    r = r - p2.astype(F32)
    return p1, p2, r.astype(BF16)


def _const_spec(shape):
    return pl.BlockSpec(shape, lambda *_: (0,) * len(shape), pipeline_mode=pl.Buffered(1))


def _inproj_kernel(x_ref, gmix_ref, w_ref, gmat_ref, qg_ref, kg_ref, bf_ref, wg_ref, bg_ref, *rest,
                   tm, t_real, seq_rows, prompt):
    i = pl.program_id(1)
    if prompt:
        (meta_ref, tri_ref, pk_ref, onek_ref, selq_ref, oneq_ref, onev_ref,
         qt_ref, ka_ref, vt_ref, ko_ref, vo_ref, lf_ref,
         gq_ref, gk_ref, gv_ref, gr_ref, gg_ref, carry_ref) = rest
        n_meta = meta_ref.shape[0]
        blk = x_ref[0]
        first = jnp.concatenate([meta_ref[...], blk[0:tm - n_meta]], axis=0)
        last = jnp.concatenate([blk[tm - n_meta:tm], jnp.zeros((tm - n_meta, D_MODEL), F32)], axis=0)
        x = jnp.where(i == 0, first, jnp.where(i == pl.num_programs(1) - 1, last, blk))
    else:
        (qc_ref, ko_ref, vo_ref, lf_ref, gq_ref, gk_ref, gv_ref, gr_ref, gg_ref) = rest
        x = x_ref[0]

    ms = jnp.mean(x * x, axis=-1, keepdims=True)
    xn = (x * lax.rsqrt(ms + EPS) * gmix_ref[...]).astype(BF16)
    z = jnp.dot(xn, w_ref[...], preferred_element_type=F32)

    def head_norm(zz, g_row):
        msq = jnp.dot((zz * zz).astype(BF16), gmat_ref[...], preferred_element_type=F32)
        return zz * lax.rsqrt(msq + EPS) * g_row

    qn = head_norm(z[:, _C_FQ:_C_FQ + FOX_DIM], qg_ref[...])
    kn = head_norm(z[:, _C_FK:_C_FK + FOX_DIM], kg_ref[...])
    zv = z[:, _C_FV:_C_FV + FOX_DIM]
    zs = z[:, _C_SM:_C_SM + SMALL_COLS]
    lf = _log_sigmoid(zs + bf_ref[...])
    if prompt:
        ko_ref[0] = kn.T
        zv_t = zv.T
        vo_ref[0] = zv_t
        lf_ref[0] = lf.T[0:FOX_HEADS, :]
    else:
        ko_ref[0] = kn
        vo_ref[0] = zv
        lf_ref[0] = lf[:, 0:FOX_HEADS]

    row = i * tm + lax.broadcasted_iota(jnp.int32, (tm, 1), 0)
    if prompt:
        valid = row < t_real
    else:
        valid = (row % seq_rows) < t_real

    gq_ref[0] = z[:, _C_GQ:_C_GQ + GLA_KDIM] * (GLA_KEY_DIM ** -0.5)
    gk_ref[0] = z[:, _C_GK:_C_GK + GLA_KDIM]
    gv_ref[0] = z[:, _C_GV:_C_GV + GLA_VDIM]
    gr_ref[0] = z[:, _C_GR:_C_GR + GLA_VDIM]
    xg = jnp.dot(zs.astype(BF16), wg_ref[...], preferred_element_type=F32) + bg_ref[...]
    gg_ref[0] = jnp.where(valid, _log_sigmoid(xg) * (1.0 / GLA_GATE_TEMP), 0.0)

    if not prompt:
        qc_ref[0] = qn.astype(BF16)
        return

    @pl.when(i == 0)
    def _():
        carry_ref[...] = jnp.zeros_like(carry_ref)

    lane = lax.broadcasted_iota(jnp.int32, (tm, SMALL_COLS), 1)
    lfm = jnp.where(lane < FOX_HEADS, lf, 0.0)
    c = jnp.dot(tri_ref[...], jnp.concatenate(_split3(lfm), axis=1), preferred_element_type=F32)
    fcum = c[:, 0:LANES] + c[:, LANES:2 * LANES] + c[:, 2 * LANES:3 * LANES] + carry_ref[...]
    carry_ref[...] = fcum[tm - 1:tm, :]
    fs = fcum * LOG2E
    a1, a2, a3 = _split3(fs)

    k_cat = jnp.concatenate([kn.astype(BF16), -a1, -a2, -a3], axis=1)
    ka = jnp.dot(k_cat, pk_ref[...], preferred_element_type=F32) + onek_ref[...]
    ka_ref[0] = ka.astype(BF16)

    qn_t = qn.T
    aug_q = jnp.dot(selq_ref[...], jnp.concatenate(_split3(fs.T), axis=0),
                    preferred_element_type=F32) + oneq_ref[...]
    zero_rows = jnp.zeros((LANES - FOX_HEAD_DIM - AUG_ROWS, tm), BF16)
    for h in range(FOX_HEADS):
        hs = slice(h * FOX_HEAD_DIM, (h + 1) * FOX_HEAD_DIM)
        qt_ref[0, h, 0:FOX_HEAD_DIM, :] = qn_t[hs].astype(BF16)
        qt_ref[0, h, FOX_HEAD_DIM:FOX_HEAD_DIM + AUG_ROWS, :] = aug_q[h * AUG_ROWS:(h + 1) * AUG_ROWS].astype(BF16)
        qt_ref[0, h, FOX_HEAD_DIM + AUG_ROWS:LANES, :] = zero_rows
        vt_ref[0, h, 0:FOX_HEAD_DIM, :] = zv_t[hs].astype(BF16)
        vt_ref[0, h, FOX_HEAD_DIM:FOX_HEAD_DIM + AUG_ROWS, :] = jnp.broadcast_to(
            onev_ref[...], (AUG_ROWS, tm)).astype(BF16)


def _placement_constants():
    n_cat = FOX_DIM + 3 * LANES
    pk = np.zeros((n_cat, AUG_DIM), np.float32)
    onek = np.zeros((1, AUG_DIM), np.float32)
    selq = np.zeros((FOX_HEADS * AUG_ROWS, 3 * LANES), np.float32)
    oneq = np.zeros((FOX_HEADS * AUG_ROWS, 1), np.float32)
    onev = np.zeros((AUG_ROWS, 1), np.float32)
    onev[0, 0] = 1.0
    for h in range(FOX_HEADS):
        base = h * LANES
        for d in range(FOX_HEAD_DIM):
            pk[h * FOX_HEAD_DIM + d, base + d] = 1.0
        for piece in range(3):
            pk[FOX_DIM + piece * LANES + h, base + FOX_HEAD_DIM + piece] = 1.0
            onek[0, base + FOX_HEAD_DIM + 3 + piece] = 1.0
            oneq[h * AUG_ROWS + piece, 0] = 1.0
            selq[h * AUG_ROWS + 3 + piece, piece * LANES + h] = 1.0
    gmat = np.kron(np.eye(FOX_HEADS, dtype=np.float32),
                   np.full((FOX_HEAD_DIM, FOX_HEAD_DIM), 1.0 / FOX_HEAD_DIM, np.float32))
    return pk, onek, selq, oneq, onev, gmat


def _inproj_weights(norm_mix_g, w_in, fox_b_f, fox_q_norm_g, fox_k_norm_g, gla_w_gup, gla_b_g):
    o = 0
    cols = {}
    for name, width in (("fq", FOX_DIM), ("fk", FOX_DIM), ("fv", FOX_DIM), ("ff", FOX_HEADS),
                        ("gq", GLA_KDIM), ("gk", GLA_KDIM), ("gv", GLA_VDIM), ("gr", GLA_VDIM),
                        ("glr", GLA_GATE_RANK)):
        cols[name] = w_in[:, o:o + width]
        o += width
    pad = jnp.zeros((D_MODEL, SMALL_COLS - FOX_HEADS - GLA_GATE_RANK), w_in.dtype)
    w = jnp.concatenate([cols["fq"], cols["fk"], cols["fv"], cols["gq"], cols["gk"], cols["gv"],
                         cols["gr"], cols["ff"], cols["glr"], pad], axis=1).astype(BF16)
    wg = jnp.zeros((SMALL_COLS, GLA_KDIM), F32).at[FOX_HEADS:FOX_HEADS + GLA_GATE_RANK].set(gla_w_gup)
    bf = jnp.zeros((1, SMALL_COLS), F32).at[0, :FOX_HEADS].set(fox_b_f)
    qg = jnp.tile(fox_q_norm_g, FOX_HEADS)[None, :] * (FOX_HEAD_DIM ** -0.5 * LOG2E)
    kg = jnp.tile(fox_k_norm_g, FOX_HEADS)[None, :]
    return dict(gmix=norm_mix_g[None, :], w=w, qg=qg, kg=kg, bf=bf, wg=wg.astype(BF16),
                bg=gla_b_g[None, :])


def _inproj(x, wts, *, t_real, prompt, seq_rows=0, meta=None):
    nb, x_rows, _ = x.shape
    tm = ROW_TILE
    row_spec = lambda width: pl.BlockSpec((1, tm, width), lambda b, i: (b, i, 0))
    if prompt:
        n_meta = meta.shape[0]
        assert x_rows % tm == 0 and t_real == n_meta + x_rows and n_meta % 16 == 0 and n_meta < tm
        rows = x_rows + tm
        x_spec = pl.BlockSpec(
            (pl.Element(1), pl.Element(tm), pl.Element(D_MODEL)),
            lambda b, i: (b, pl.multiple_of(jnp.clip(i * tm - n_meta, 0, x_rows - tm), 16), 0))
    else:
        rows = x_rows
        x_spec = row_spec(D_MODEL)
    nt = rows // tm
    pk, onek, selq, oneq, onev, gmat = _placement_constants()
    in_specs = [x_spec, _const_spec((1, D_MODEL)), _const_spec((D_MODEL, _W_COLS)),
                _const_spec((FOX_DIM, FOX_DIM)), _const_spec((1, FOX_DIM)), _const_spec((1, FOX_DIM)),
                _const_spec((1, SMALL_COLS)), _const_spec((SMALL_COLS, GLA_KDIM)),
                _const_spec((1, GLA_KDIM))]
    args = [x, wts["gmix"], wts["w"], jnp.asarray(gmat, BF16), wts["qg"], wts["kg"], wts["bf"],
            wts["wg"], wts["bg"]]
    gla_shapes = [jax.ShapeDtypeStruct((nb, rows, GLA_KDIM), F32),
                  jax.ShapeDtypeStruct((nb, rows, GLA_KDIM), F32),
                  jax.ShapeDtypeStruct((nb, rows, GLA_VDIM), F32),
                  jax.ShapeDtypeStruct((nb, rows, GLA_VDIM), F32),
                  jax.ShapeDtypeStruct((nb, rows, GLA_KDIM), F32)]
    gla_specs = [row_spec(GLA_KDIM), row_spec(GLA_KDIM), row_spec(GLA_VDIM), row_spec(GLA_VDIM),
                 row_spec(GLA_KDIM)]
    scratch = []
    if prompt:
        n_cat = FOX_DIM + 3 * LANES
        tri = np.tril(np.ones((tm, tm), np.float32))
        in_specs += [_const_spec((n_meta, D_MODEL)),
                     _const_spec((tm, tm)), _const_spec((n_cat, AUG_DIM)), _const_spec((1, AUG_DIM)),
                     _const_spec((FOX_HEADS * AUG_ROWS, 3 * LANES)), _const_spec((FOX_HEADS * AUG_ROWS, 1)),
                     _const_spec((AUG_ROWS, 1))]
        args += [meta, jnp.asarray(tri, BF16), jnp.asarray(pk, BF16), jnp.asarray(onek),
                 jnp.asarray(selq, BF16), jnp.asarray(oneq), jnp.asarray(onev)]
        head_spec = lambda height: pl.BlockSpec((1, FOX_HEADS, height, tm), lambda b, i: (b, 0, 0, i))
        out_shape = [jax.ShapeDtypeStruct((nb, FOX_HEADS, LANES, rows), BF16),
                     jax.ShapeDtypeStruct((nb, rows, AUG_DIM), BF16),
                     jax.ShapeDtypeStruct((nb, FOX_HEADS, V_ROWS, rows), BF16)]
        out_specs = [head_spec(LANES), row_spec(AUG_DIM), head_spec(V_ROWS)]
        col_spec = lambda height: pl.BlockSpec((1, height, tm), lambda b, i: (b, 0, i))
        out_shape += [jax.ShapeDtypeStruct((nb, FOX_DIM, t_real), F32),
                      jax.ShapeDtypeStruct((nb, FOX_DIM, t_real), F32),
                      jax.ShapeDtypeStruct((nb, FOX_HEADS, t_real), F32)]
        out_specs += [col_spec(FOX_DIM), col_spec(FOX_DIM), col_spec(FOX_HEADS)]
        scratch = [pltpu.VMEM((1, SMALL_COLS), F32)]
    else:
        out_shape = [jax.ShapeDtypeStruct((nb, rows, FOX_DIM), BF16),
                     jax.ShapeDtypeStruct((nb, rows, FOX_DIM), F32),
                     jax.ShapeDtypeStruct((nb, rows, FOX_DIM), F32),
                     jax.ShapeDtypeStruct((nb, rows, FOX_HEADS), F32)]
        out_specs = [row_spec(FOX_DIM), row_spec(FOX_DIM), row_spec(FOX_DIM), row_spec(FOX_HEADS)]
    out_shape += gla_shapes
    out_specs += gla_specs
    return pl.pallas_call(
        functools.partial(_inproj_kernel, tm=tm, t_real=t_real, seq_rows=seq_rows, prompt=prompt),
        grid=(nb, nt),
        in_specs=in_specs,
        out_specs=out_specs,
        out_shape=out_shape,
        scratch_shapes=scratch,
        compiler_params=pltpu.CompilerParams(dimension_semantics=("arbitrary", "arbitrary"),
                                             vmem_limit_bytes=VMEM_LIMIT),
        name="inproj_prompt" if prompt else "inproj_decode",
    )(*args)


def _fox_prompt_kernel(qt_ref, k_ref, vt_ref, o_ref, *scratch, tile, heads):
    i = pl.program_id(2)
    per_head = [scratch[6 * hs:6 * hs + 6] for hs in range(heads)]
    for _, _, _, _, m_ref, acc_ref in per_head:
        m_ref[...] = jnp.full_like(m_ref, -jnp.inf)
        acc_ref[...] = jnp.zeros_like(acc_ref)

    def scores(t, parity, causal):
        for hs in range(heads):
            s_ref, mt_ref = per_head[hs][parity], per_head[hs][2 + parity]
            k = k_ref[0, pl.ds(pl.multiple_of(t * tile, tile), tile), hs * LANES:(hs + 1) * LANES]
            s = jnp.dot(k, qt_ref[0, hs], preferred_element_type=F32)
            if causal:
                r = lax.broadcasted_iota(jnp.int32, (tile, tile), 0)
                c = lax.broadcasted_iota(jnp.int32, (tile, tile), 1)
                s = jnp.where(r <= c, s, -jnp.inf)
            s_ref[...] = s
            mt_ref[...] = jnp.max(s, axis=0, keepdims=True)

    def accumulate(t, parity):
        for hs in range(heads):
            s_ref, mt_ref = per_head[hs][parity], per_head[hs][2 + parity]
            m_ref, acc_ref = per_head[hs][4], per_head[hs][5]
            vt = vt_ref[0, hs, :, pl.ds(pl.multiple_of(t * tile, tile), tile)]
            m_old = m_ref[...]
            m_new = jnp.maximum(m_old, mt_ref[...])
            p = jnp.exp2(s_ref[...] - m_new)
            acc_ref[...] = acc_ref[...] * jnp.exp2(m_old - m_new) + jnp.dot(
                vt, p.astype(BF16), preferred_element_type=F32)
            m_ref[...] = m_new

    def step(t, parity, causal_next):
        scores(t + 1, 1 - parity, causal_next)
        accumulate(t, parity)

    odd = (i % 2) == 1

    @pl.when(i == 0)
    def _():
        scores(0, 0, True)
        accumulate(0, 0)

    @pl.when(i > 0)
    def _():
        scores(0, 0, False)
        unroll = PROMPT_TILES_PER_TRIP
        n_trips = (i - 1) // unroll

        def trip(jj, carry):
            for u in range(unroll):
                step(unroll * jj + u, u % 2, False)
            return carry

        lax.fori_loop(0, n_trips, trip, 0)
        t0 = unroll * n_trips
        pre = i - t0 - jnp.where(odd, 1, 2)

        @pl.when(pre >= 4)
        def _():
            for u in range(4):
                step(t0 + u, u % 2, False)

        t1 = t0 + jnp.where(pre >= 4, 4, 0)

        @pl.when(pre % 4 == 2)
        def _():
            step(t1, 0, False)
            step(t1 + 1, 1, False)

        @pl.when(odd)
        def _():
            step(i - 1, 0, True)
            accumulate(i, 1)

        @pl.when(jnp.logical_not(odd))
        def _():
            step(i - 2, 0, False)
            step(i - 1, 1, True)
            accumulate(i, 0)

    for hs in range(heads):
        acc = per_head[hs][5][...]
        o_t = acc[0:FOX_HEAD_DIM] / acc[FOX_HEAD_DIM:FOX_HEAD_DIM + 1]
        o_t = jnp.concatenate([o_t, jnp.zeros((LANES - FOX_HEAD_DIM, tile), F32)], axis=0)
        o_ref[0, :, hs * LANES:(hs + 1) * LANES] = o_t.T.astype(BF16)


def _fox_prompt(qt, ka, vt):
    nb, rows, _ = ka.shape
    tile = ROW_TILE
    nq = rows // tile
    hps = PROMPT_HEADS_PER_STEP
    per_head_scratch = [pltpu.VMEM((tile, tile), F32), pltpu.VMEM((tile, tile), F32),
                        pltpu.VMEM((1, tile), F32), pltpu.VMEM((1, tile), F32),
                        pltpu.VMEM((1, tile), F32), pltpu.VMEM((V_ROWS, tile), F32)]
    return pl.pallas_call(
        functools.partial(_fox_prompt_kernel, tile=tile, heads=hps),
        grid=(nb, FOX_HEADS // hps, nq),
        in_specs=[pl.BlockSpec((1, hps, LANES, tile), lambda b, h, i: (b, h, 0, i)),
                  pl.BlockSpec((1, rows, hps * LANES), lambda b, h, i: (b, 0, h)),
                  pl.BlockSpec((1, hps, V_ROWS, rows), lambda b, h, i: (b, h, 0, 0))],
        out_specs=pl.BlockSpec((1, tile, hps * LANES), lambda b, h, i: (b, i, h)),
        out_shape=jax.ShapeDtypeStruct((nb, rows, AUG_DIM), BF16),
        scratch_shapes=per_head_scratch * hps,
        compiler_params=pltpu.CompilerParams(
            dimension_semantics=("arbitrary", "arbitrary", "arbitrary"), vmem_limit_bytes=VMEM_LIMIT),
        name="fox_prompt",
    )(qt, ka, vt)


def _fox_decode_kernel(pt_ref, qbd_ref, kn_ref, vn_ref, lfn_ref, usuf_ref, *rest, n_pages, t_new):
    del pt_ref
    p = n_pages
    k_refs, v_refs, lf_refs = rest[0:p], rest[p:2 * p], rest[2 * p:3 * p]
    o_ref, m_ref, l_ref, acc_ref, carry_ref, qoff_ref = rest[3 * p:]
    j = pl.program_id(1)
    nq = t_new * FOX_HEADS
    qbd = qbd_ref[0]

    def suffix_sums(lf, token_minor):
        n = lf.shape[1] if token_minor else lf.shape[0]
        u = usuf_ref[0:n, :]
        u = jnp.concatenate([u[:, 0:n], u[:, PAGE_SIZE:PAGE_SIZE + LANES]], axis=1)
        if token_minor:
            r = jnp.dot(jnp.concatenate(_split3(lf), axis=0), u, preferred_element_type=F32)
        else:
            r = lax.dot_general(jnp.concatenate(_split3(lf), axis=1), u, (((0,), (0,)), ((), ())),
                                preferred_element_type=F32)
        r = r[0:FOX_HEADS] + r[FOX_HEADS:2 * FOX_HEADS] + r[2 * FOX_HEADS:3 * FOX_HEADS]
        return r[:, 0:n], r[:, n:n + LANES]

    def attend(k, v, bias_t, causal, token_minor):
        k = k.astype(BF16)
        v = v.astype(BF16)
        if token_minor:
            n = k.shape[1]
            s = jnp.dot(qbd, k, preferred_element_type=F32)
        else:
            n = k.shape[0]
            s = lax.dot_general(qbd, k, (((1,), (1,)), ((), ())), preferred_element_type=F32)
        s = s + jnp.concatenate([bias_t] * t_new, axis=0) - qoff_ref[...]
        if causal:
            r = lax.broadcasted_iota(jnp.int32, (nq, n), 0)
            c = lax.broadcasted_iota(jnp.int32, (nq, n), 1)
            s = jnp.where(c * FOX_HEADS <= r, s, -jnp.inf)
        m_old = m_ref[...]
        m_new = jnp.maximum(m_old, jnp.max(s, axis=-1, keepdims=True))
        alpha = jnp.exp2(m_old - m_new)
        pr = jnp.exp2(s - m_new)
        l_ref[...] = l_ref[...] * alpha + jnp.sum(pr, axis=-1, keepdims=True)
        pv_dims = (((1,), (1,)), ((), ())) if token_minor else (((1,), (0,)), ((), ()))
        acc_ref[...] = acc_ref[...] * alpha + lax.dot_general(pr.astype(BF16), v, pv_dims,
                                                              preferred_element_type=F32)
        m_ref[...] = m_new

    @pl.when(j == 0)
    def _():
        m_ref[...] = jnp.full_like(m_ref, -jnp.inf)
        l_ref[...] = jnp.zeros_like(l_ref)
        acc_ref[...] = jnp.zeros_like(acc_ref)
        rowi = lax.broadcasted_iota(jnp.int32, (DECODE_ROWS, FOX_HEADS), 0)
        lf = jnp.where(rowi < t_new, lfn_ref[0], 0.0) * LOG2E
        r_t, tot = suffix_sums(lf, False)
        rt_rows = jnp.concatenate([r_t] * t_new, axis=0)
        r = lax.broadcasted_iota(jnp.int32, (nq, DECODE_ROWS), 0)
        c = lax.broadcasted_iota(jnp.int32, (nq, DECODE_ROWS), 1)
        own_t = jnp.logical_and(c * FOX_HEADS <= r, r < (c + 1) * FOX_HEADS)
        qoff_ref[...] = jnp.sum(jnp.where(own_t, rt_rows, 0.0), axis=-1, keepdims=True)
        attend(kn_ref[0], vn_ref[0], r_t, True, False)
        carry_ref[...] = tot

    n_piece_rows = 3 * FOX_HEADS
    lf_stack = jnp.concatenate(
        [piece for idx in range(p) for piece in _split3(lf_refs[idx][...] * LOG2E)], axis=0)
    r_all = jnp.dot(lf_stack, usuf_ref[...], preferred_element_type=F32)
    carry = carry_ref[...]
    biases = []
    for idx in range(p):
        r = r_all[idx * n_piece_rows:(idx + 1) * n_piece_rows]
        r = r[0:FOX_HEADS] + r[FOX_HEADS:2 * FOX_HEADS] + r[2 * FOX_HEADS:3 * FOX_HEADS]
        biases.append(r[:, 0:PAGE_SIZE] + carry)
        carry = carry + r[:, PAGE_SIZE:PAGE_SIZE + LANES]
    carry_ref[...] = carry
    attend(jnp.concatenate([k_refs[idx][...].astype(BF16) for idx in range(p)], axis=1),
           jnp.concatenate([v_refs[idx][...].astype(BF16) for idx in range(p)], axis=1),
           jnp.concatenate(biases, axis=1), False, True)

    @pl.when(j == pl.num_programs(1) - 1)
    def _():
        o = acc_ref[...] / l_ref[...]
        r = lax.broadcasted_iota(jnp.int32, (nq, FOX_DIM), 0)
        c = lax.broadcasted_iota(jnp.int32, (nq, FOX_DIM), 1)
        own = (c // FOX_HEAD_DIM) == (r % FOX_HEADS)
        o = jnp.sum(jnp.where(own, o, 0.0).reshape(t_new, FOX_HEADS, FOX_DIM), axis=1)
        o = jnp.concatenate([o, jnp.zeros((DECODE_ROWS - t_new, FOX_DIM), F32)], axis=0)
        o_ref[0] = o.astype(BF16)


def _fox_decode(page_table, qbd, k_new, v_new, lf_new, cache_k, cache_v, cache_logf, *, t_new):
    n_seq, n_pages_seq = page_table.shape
    p = DECODE_PAGES_PER_STEP
    n_steps = n_pages_seq // p
    nq = t_new * FOX_HEADS
    n_pool = cache_k.shape[0]
    ck = jnp.transpose(cache_k, (0, 2, 3, 1)).reshape(n_pool, FOX_DIM, PAGE_SIZE)
    cv = jnp.transpose(cache_v, (0, 2, 3, 1)).reshape(n_pool, FOX_DIM, PAGE_SIZE)
    clf = jnp.transpose(cache_logf, (0, 2, 1))
    usuf = np.concatenate([np.tril(np.ones((PAGE_SIZE, PAGE_SIZE), np.float32), -1),
                           np.ones((PAGE_SIZE, LANES), np.float32)], axis=1)

    def page_map(idx):
        return lambda n, j, pt: (pt[n * n_pages_seq + n_pages_seq - 1 - (j * p + idx)], 0, 0)

    seq_spec = lambda r, w: pl.BlockSpec((1, r, w), lambda n, j, pt: (n, 0, 0))
    in_specs = [seq_spec(nq, FOX_DIM), seq_spec(DECODE_ROWS, FOX_DIM), seq_spec(DECODE_ROWS, FOX_DIM),
                seq_spec(DECODE_ROWS, FOX_HEADS),
                pl.BlockSpec((PAGE_SIZE, PAGE_SIZE + LANES), lambda n, j, pt: (0, 0))]
    in_specs += [pl.BlockSpec((None, FOX_DIM, PAGE_SIZE), page_map(idx)) for idx in range(p)]
    in_specs += [pl.BlockSpec((None, FOX_DIM, PAGE_SIZE), page_map(idx)) for idx in range(p)]
    in_specs += [pl.BlockSpec((None, FOX_HEADS, PAGE_SIZE), page_map(idx)) for idx in range(p)]
    grid_spec = pltpu.PrefetchScalarGridSpec(
        num_scalar_prefetch=1,
        grid=(n_seq, n_steps),
        in_specs=in_specs,
        out_specs=pl.BlockSpec((1, DECODE_ROWS, FOX_DIM), lambda n, j, pt: (n, 0, 0)),
        scratch_shapes=[pltpu.VMEM((nq, 1), F32), pltpu.VMEM((nq, 1), F32), pltpu.VMEM((nq, FOX_DIM), F32),
                        pltpu.VMEM((FOX_HEADS, LANES), F32), pltpu.VMEM((nq, 1), F32)],
    )
    return pl.pallas_call(
        functools.partial(_fox_decode_kernel, n_pages=p, t_new=t_new),
        grid_spec=grid_spec,
        out_shape=jax.ShapeDtypeStruct((n_seq, DECODE_ROWS, FOX_DIM), BF16),
        compiler_params=pltpu.CompilerParams(dimension_semantics=("arbitrary", "arbitrary"),
                                             vmem_limit_bytes=VMEM_LIMIT),
        name="fox_decode",
    )(page_table.reshape(-1), qbd, k_new, v_new, lf_new, jnp.asarray(usuf, BF16),
      *([ck] * p), *([cv] * p), *([clf] * p))


def _gla_kernel(gq_ref, gk_ref, gv_ref, gr_ref, gg_ref, s0_ref, gn_ref, tri_ref,
                o_ref, sfin_ref, st_ref, *, tm, chunk, seqs):
    i = pl.program_id(1)
    n_pairs = GLA_HEADS // 2
    pair = 2 * GLA_KEY_DIM

    @pl.when(i == 0)
    def _():
        for sq in range(seqs):
            for hp in range(n_pairs):
                st_ref[sq, hp] = s0_ref[sq, 2 * hp:2 * hp + 2].reshape(pair, GLA_VAL_DIM).T

    nch = tm // chunk
    lane = lax.broadcasted_iota(jnp.int32, (tm, pair), 1)
    rr = lax.broadcasted_iota(jnp.int32, (tm, tm), 0)
    cc = lax.broadcasted_iota(jnp.int32, (tm, tm), 1)
    in_chunk = jnp.logical_and(cc <= rr, cc >= rr - rr % chunk)
    tri = tri_ref[...]
    gn = gn_ref[...]
    by_chunk = lambda a: a.reshape(nch, chunk, a.shape[-1])

    for sq, hp in [(a, b) for a in range(seqs) for b in range(n_pairs)]:
        ks = slice(hp * pair, (hp + 1) * pair)
        g = gg_ref[sq, :, ks]
        q = gq_ref[sq, :, ks]
        k = gk_ref[sq, :, ks]
        cb = jnp.dot(tri, jnp.concatenate(_split3(g), axis=1), preferred_element_type=F32)
        bc = cb[:, 0:pair] + cb[:, pair:2 * pair] + cb[:, 2 * pair:3 * pair]
        bc3 = by_chunk(bc)
        b_last = bc3[:, chunk - 1:chunk, :]
        qt = q * jnp.exp(bc)
        kt = (k * jnp.exp(-bc)).astype(BF16)
        kh = (by_chunk(k) * jnp.exp(b_last - bc3)).reshape(tm, pair)
        decay = jnp.exp(b_last)
        heads = []
        upd = None
        for hh in range(2):
            own = (lane // GLA_KEY_DIM) == hh
            vs = slice((2 * hp + hh) * GLA_VAL_DIM, (2 * hp + hh + 1) * GLA_VAL_DIM)
            v = gv_ref[sq, :, vs].astype(BF16)
            qm = jnp.where(own, qt, 0.0).astype(BF16)
            khm = jnp.where(own, kh, 0.0).astype(BF16)
            u = lax.dot_general(by_chunk(v), by_chunk(khm), (((1,), (1,)), ((0,), (0,))),
                                preferred_element_type=F32)
            upd = u if upd is None else upd + u
            heads.append((vs, v, qm))
        st = st_ref[sq, hp]
        entering = []
        for c in range(nch):
            entering.append(st)
            st = st * decay[c] + upd[c]
        st_ref[sq, hp] = st
        st_in = jnp.stack(entering).astype(BF16)
        for vs, v, qm in heads:
            a = lax.dot_general(qm, kt, (((1,), (1,)), ((), ())), preferred_element_type=F32)
            a = jnp.where(in_chunk, a, 0.0).astype(BF16)
            o = jnp.dot(a, v, preferred_element_type=F32)
            o = o + lax.dot_general(by_chunk(qm), st_in, (((2,), (2,)), ((0,), (0,))),
                                    preferred_element_type=F32).reshape(tm, GLA_VAL_DIM)
            on = o * lax.rsqrt(jnp.mean(o * o, axis=-1, keepdims=True) + EPS) * gn
            gate = gr_ref[sq, :, vs]
            gate = gate * (1.0 / (1.0 + jnp.exp(-gate)))
            o_ref[sq, :, vs] = (on * gate).astype(BF16)

    @pl.when(i == pl.num_programs(1) - 1)
    def _():
        for sq in range(seqs):
            for hp in range(n_pairs):
                sfin_ref[sq, 2 * hp:2 * hp + 2] = st_ref[sq, hp].T.reshape(2, GLA_KEY_DIM, GLA_VAL_DIM)


def _gla(gq, gk, gv, gr, gg, s0, gla_out_norm_g, *, tm, chunk, seqs):
    nb, rows, _ = gq.shape
    nt = rows // tm
    assert nb % seqs == 0
    tri = np.kron(np.eye(tm // chunk, dtype=np.float32), np.tril(np.ones((chunk, chunk), np.float32)))
    row_spec = lambda width: pl.BlockSpec((seqs, tm, width), lambda b, i: (b, i, 0))
    state_spec = pl.BlockSpec((seqs, GLA_HEADS, GLA_KEY_DIM, GLA_VAL_DIM), lambda b, i: (b, 0, 0, 0))
    return pl.pallas_call(
        functools.partial(_gla_kernel, tm=tm, chunk=chunk, seqs=seqs),
        grid=(nb // seqs, nt),
        in_specs=[row_spec(GLA_KDIM), row_spec(GLA_KDIM), row_spec(GLA_VDIM), row_spec(GLA_VDIM),
                  row_spec(GLA_KDIM), state_spec,
                  pl.BlockSpec((1, GLA_VAL_DIM), lambda b, i: (0, 0)),
                  pl.BlockSpec((tm, tm), lambda b, i: (0, 0))],
        out_specs=[row_spec(GLA_VDIM), state_spec],
        out_shape=[jax.ShapeDtypeStruct((nb, rows, GLA_VDIM), BF16),
                   jax.ShapeDtypeStruct((nb, GLA_HEADS, GLA_KEY_DIM, GLA_VAL_DIM), F32)],
        scratch_shapes=[pltpu.VMEM((seqs, GLA_HEADS // 2, GLA_VAL_DIM, 2 * GLA_KEY_DIM), F32)],
        compiler_params=pltpu.CompilerParams(dimension_semantics=("arbitrary", "arbitrary"),
                                             vmem_limit_bytes=VMEM_LIMIT),
        name="gla",
    )(gq, gk, gv, gr, gg, s0, gla_out_norm_g[None, :], jnp.asarray(tri, BF16))


def _out_mlp_kernel(x_ref, of_ref, og_ref, wof_ref, wog_ref, gm_ref, wup_ref, wdn_ref, y_ref, *, ff_chunk):
    x = x_ref[0]
    h = (x + jnp.dot(of_ref[0], wof_ref[...], preferred_element_type=F32)
         + jnp.dot(og_ref[0], wog_ref[...], preferred_element_type=F32))
    ms = jnp.mean(h * h, axis=-1, keepdims=True)
    hn = (h * lax.rsqrt(ms + EPS) * gm_ref[...]).astype(BF16)
    y_ref[0] = h
    for c in range(D_FF // ff_chunk):
        cs = slice(c * ff_chunk, (c + 1) * ff_chunk)
        u = jnp.maximum(jnp.dot(hn, wup_ref[:, cs], preferred_element_type=F32), 0.0)
        y_ref[0] += jnp.dot((u * u).astype(BF16), wdn_ref[cs, :], preferred_element_type=F32)


def _out_mlp(x, o_fox, o_gla, wof, wog, norm_mlp_g, w_up, w_down, *, fox_row_offset=0):
    nb, rows, _ = x.shape
    tm = ROW_TILE
    nt = rows // tm
    kf = o_fox.shape[-1]
    if fox_row_offset:
        assert fox_row_offset % 16 == 0, "row offset must keep bf16 sublane tiles aligned"
        shifted = lambda b, i: (b, pl.multiple_of(i * tm + fox_row_offset, 16), 0)
        of_spec = pl.BlockSpec((pl.Element(1), pl.Element(tm), pl.Element(kf)), shifted)
        og_spec = pl.BlockSpec((pl.Element(1), pl.Element(tm), pl.Element(GLA_VDIM)), shifted)
    else:
        of_spec = pl.BlockSpec((1, tm, kf), lambda b, i: (b, i, 0))
        og_spec = pl.BlockSpec((1, tm, GLA_VDIM), lambda b, i: (b, i, 0))
    return pl.pallas_call(
        functools.partial(_out_mlp_kernel, ff_chunk=D_MODEL),
        grid=(nb, nt),
        in_specs=[pl.BlockSpec((1, tm, D_MODEL), lambda b, i: (b, i, 0)), of_spec, og_spec,
                  _const_spec((kf, D_MODEL)), _const_spec((GLA_VDIM, D_MODEL)), _const_spec((1, D_MODEL)),
                  _const_spec((D_MODEL, D_FF)), _const_spec((D_FF, D_MODEL))],
        out_specs=pl.BlockSpec((1, tm, D_MODEL), lambda b, i: (b, i, 0)),
        out_shape=jax.ShapeDtypeStruct((nb, rows, D_MODEL), F32),
        compiler_params=pltpu.CompilerParams(dimension_semantics=("arbitrary", "arbitrary"),
                                             vmem_limit_bytes=VMEM_LIMIT),
        name="out_mlp",
    )(x, o_fox, o_gla, wof, wog, norm_mlp_g[None, :], w_up, w_down)


def kernel(x_prompt, x_sample, cache_k, cache_v, cache_logf, state_gla, page_table, meta_tokens, norm_mix_g,
           w_in, fox_b_f, fox_q_norm_g, fox_k_norm_g, gla_w_gup, gla_b_g, gla_out_norm_g, w_out, norm_mlp_g,
           w_up, w_down):
    depth = w_in.shape[0]
    assert depth == 1, "single-layer trunk"
    nb, seq, _ = x_prompt.shape
    n_seq, dec_seq, _ = x_sample.shape
    t_real = N_META + seq

    wts = _inproj_weights(norm_mix_g[0], w_in[0], fox_b_f[0], fox_q_norm_g[0], fox_k_norm_g[0],
                          gla_w_gup[0], gla_b_g[0])
    w_out_l = w_out[0]
    wof_aug = jnp.zeros((FOX_HEADS, LANES, D_MODEL), F32).at[:, :FOX_HEAD_DIM].set(
        w_out_l[:FOX_DIM].reshape(FOX_HEADS, FOX_HEAD_DIM, D_MODEL)).reshape(AUG_DIM, D_MODEL).astype(BF16)
    wof = w_out_l[:FOX_DIM].astype(BF16)
    wog = w_out_l[FOX_DIM:].astype(BF16)
    w_up_b = w_up[0].astype(BF16)
    w_dn_b = w_down[0].astype(BF16)

    qt, ka, vt, k_p, v_p, lf_p, gq, gk, gv, gr, gg = _inproj(
        x_prompt, wts, t_real=t_real, prompt=True, meta=meta_tokens.astype(x_prompt.dtype))
    o_fox = _fox_prompt(qt, ka, vt)
    s0 = jnp.zeros((nb, GLA_HEADS, GLA_KEY_DIM, GLA_VAL_DIM), F32)
    o_gla, s_p = _gla(gq, gk, gv, gr, gg, s0, gla_out_norm_g[0], tm=ROW_TILE, chunk=GLA_CHUNK, seqs=nb)
    y_p = _out_mlp(x_prompt, o_fox, o_gla, wof_aug, wog, norm_mlp_g[0], w_up_b, w_dn_b,
                   fox_row_offset=N_META)

    xs = jnp.pad(x_sample, ((0, 0), (0, DECODE_ROWS - dec_seq), (0, 0))).reshape(1, n_seq * DECODE_ROWS, D_MODEL)
    qc, k_s, v_s, lf_s, hq, hk, hv, hr, hg = _inproj(xs, wts, t_real=dec_seq, prompt=False,
                                                      seq_rows=DECODE_ROWS)
    per_seq = lambda a: a.reshape(n_seq, DECODE_ROWS, a.shape[-1])
    head_of_lane = jnp.arange(FOX_DIM) // FOX_HEAD_DIM
    qmask = (head_of_lane[None, :] == jnp.arange(FOX_HEADS)[:, None]).astype(BF16)
    qbd = (per_seq(qc)[:, :dec_seq, None, :] * qmask[None, None]).reshape(n_seq, dec_seq * FOX_HEADS, FOX_DIM)
    o_fox_s = _fox_decode(page_table, qbd, per_seq(k_s), per_seq(v_s), per_seq(lf_s),
                          cache_k[0], cache_v[0], cache_logf[0], t_new=dec_seq)
    o_gla_s, s_s = _gla(per_seq(hq), per_seq(hk), per_seq(hv), per_seq(hr), per_seq(hg),
                        state_gla[0].astype(F32), gla_out_norm_g[0], tm=DECODE_ROWS, chunk=DECODE_ROWS,
                        seqs=GLA_DECODE_SEQS_PER_STEP)
    y_s = _out_mlp(xs, o_fox_s.reshape(1, n_seq * DECODE_ROWS, FOX_DIM),
                   o_gla_s.reshape(1, n_seq * DECODE_ROWS, GLA_VDIM), wof, wog, norm_mlp_g[0], w_up_b, w_dn_b)

    real = lambda a: per_seq(a[0])[:, :dec_seq]
    return (y_p,
            real(y_s),
            jnp.transpose(k_p.reshape(1, nb, FOX_HEADS, FOX_HEAD_DIM, t_real), (0, 1, 4, 2, 3)),
            jnp.transpose(v_p.reshape(1, nb, FOX_HEADS, FOX_HEAD_DIM, t_real), (0, 1, 4, 2, 3)),
            jnp.transpose(lf_p, (0, 2, 1))[None],
            s_p[None],
            real(k_s).reshape(1, n_seq, dec_seq, FOX_HEADS, FOX_HEAD_DIM),
            real(v_s).reshape(1, n_seq, dec_seq, FOX_HEADS, FOX_HEAD_DIM),
            real(lf_s)[None],
            s_s[None])
```
